```python
import jax, jax.numpy as jnp
from jax import lax
import numpy as np

D_MODEL = 1024
BATCH = 16
SEQ = 2048
DEPTH = 2
DEC_BATCH = 32
DEC_SEQ = 8
PAST_LEN = 16384
PAGE_SIZE = 128

HEAD_DIM = 64
HEADS_PER_GROUP = 8
ATTN_GROUPS = ((128, 1), (512, 4), (2048, 16))
N_GROUPS = len(ATTN_GROUPS)
ATTN_HEADS = N_GROUPS * HEADS_PER_GROUP
QKV_W = ATTN_HEADS * HEAD_DIM
ATTN_OUT_W = HEADS_PER_GROUP * HEAD_DIM
ATTN_BLOCK = 128
ATTN_SCALE = HEAD_DIM ** -0.5
CONV_W = D_MODEL // 2
CONV_WIDTH = 31
IN_W = 3 * QKV_W + 2 * CONV_W + 2 * D_MODEL
N_KEYS = 128
N_EXPERTS = N_KEYS * N_KEYS
PEER_HEADS = 8
PEER_TOPK = 16
PK_DIM = 128
PEER_Q_W = PEER_HEADS * 2 * PK_DIM
PEER_BLOCK = 128
NORM_EPS = 1e-6
NEG_INF = -1e30

kernel_name = "dilated_swa_conformer_peer_hybrid_step"


def rmsnorm(x, g):
    xf = x.astype(jnp.float32)
    y = xf * lax.rsqrt(jnp.mean(xf * xf, axis=-1, keepdims=True) + NORM_EPS)
    return (y * g.astype(jnp.float32)).astype(x.dtype)


def layernorm(x, g, b):
    xf = x.astype(jnp.float32)
    mu = jnp.mean(xf, axis=-1, keepdims=True)
    var = jnp.mean(jnp.square(xf - mu), axis=-1, keepdims=True)
    y = (xf - mu) * lax.rsqrt(var + NORM_EPS)
    return (y * g.astype(jnp.float32) + b.astype(jnp.float32)).astype(x.dtype)


def masked_attend(scores, mask, v, spec):
    scores = jnp.where(mask, scores, NEG_INF)
    m = jnp.max(scores, axis=-1, keepdims=True)
    p = jnp.exp(scores - m)
    den = jnp.sum(p, axis=-1, keepdims=True)
    o = jnp.einsum(spec, (p / den).astype(v.dtype), v)
    lse = (m + jnp.log(den))[..., 0]
    return o, lse


def dilated_attn_prompt(q, k, v, window, dilation):
    b, s, h, hd = q.shape
    span = window // dilation
    sd = s // dilation
    nb = -(-sd // ATTN_BLOCK)
    pad_end = nb * ATTN_BLOCK - sd
    n = b * dilation

    def classes(a):
        return a.reshape(b, sd, dilation, h, hd).transpose(0, 2, 1, 3, 4).reshape(n, sd, h, hd)

    qc = jnp.pad(classes(q), ((0, 0), (0, pad_end), (0, 0), (0, 0))).reshape(n, nb, ATTN_BLOCK, h, hd)
    kc = jnp.pad(classes(k), ((0, 0), (ATTN_BLOCK, pad_end), (0, 0), (0, 0))).reshape(n, nb + 1, ATTN_BLOCK, h, hd)
    vc = jnp.pad(classes(v), ((0, 0), (ATTN_BLOCK, pad_end), (0, 0), (0, 0))).reshape(n, nb + 1, ATTN_BLOCK, h, hd)
    ks = jnp.concatenate([kc[:, :-1], kc[:, 1:]], axis=2)
    vs = jnp.concatenate([vc[:, :-1], vc[:, 1:]], axis=2)
    scores = jnp.einsum('nbqhc,nbkhc->nbhqk', qc, ks, preferred_element_type=jnp.float32) * ATTN_SCALE
    qi = jnp.arange(ATTN_BLOCK)[:, None]
    kj = jnp.arange(2 * ATTN_BLOCK)[None, :]
    dist = qi + ATTN_BLOCK - kj
    kpos = jnp.arange(nb)[:, None, None] * ATTN_BLOCK + kj[None] - ATTN_BLOCK
    mask = (dist >= 0) & (dist <= span) & (kpos >= 0)
    o, lse = masked_attend(scores, mask[None, :, None], vs, 'nbhqk,nbkhc->nbqhc')
    o = o.reshape(n, nb * ATTN_BLOCK, h, hd)[:, :sd]
    o = o.reshape(b, dilation, sd, h, hd).transpose(0, 2, 1, 3, 4).reshape(b, s, h, hd)
    lse = lse.transpose(0, 1, 3, 2).reshape(n, nb * ATTN_BLOCK, h)[:, :sd]
    lse = lse.reshape(b, dilation, sd, h).transpose(0, 2, 1, 3).reshape(b, s, h)
    return o, lse


def dilated_attn_sample(q, k_new, v_new, kv_buf, window, dilation):
    b, n, h, hd = q.shape
    span = window // dilation
    buf_len = kv_buf.shape[1]
    k_all = jnp.concatenate([kv_buf[:, :, 0], k_new], axis=1)
    v_all = jnp.concatenate([kv_buf[:, :, 1], v_new], axis=1)
    idx = buf_len + jnp.arange(n)[:, None] - jnp.arange(span + 1)[None, :] * dilation
    mask = idx >= 0
    idx = jnp.maximum(idx, 0)
    kg = k_all[:, idx]
    vg = v_all[:, idx]
    scores = jnp.einsum('bqhc,bqjhc->bhqj', q, kg, preferred_element_type=jnp.float32) * ATTN_SCALE
    o, lse = masked_attend(scores, mask[None, None], vg, 'bhqj,bqjhc->bqhc')
    keep = min(window, buf_len + n)
    new_buf = jnp.stack([k_all[:, -keep:], v_all[:, -keep:]], axis=2)
    return o, lse.transpose(0, 2, 1), new_buf


def conv_branch(glu, conv_state, dw_kernel, dw_bias, ln_g, ln_b, w_conv_out, b_conv_out):
    a, gt = glu[..., :CONV_W], glu[..., CONV_W:]
    c = a * jax.nn.sigmoid(gt)
    c_ext = jnp.concatenate([conv_state, c], axis=1)
    new_state = c_ext[:, -(CONV_WIDTH - 1):]
    y = lax.conv_general_dilated(c_ext, dw_kernel[:, None, :], window_strides=(1,), padding='VALID',
                                 dimension_numbers=('NWC', 'WIO', 'NWC'), feature_group_count=CONV_W)
    y = layernorm(y + dw_bias, ln_g, ln_b)
    y = jax.nn.silu(y)
    return y @ w_conv_out + b_conv_out, new_state


def mixer(h, kv_bufs, conv_state, w_in, dw_kernel, dw_bias, ln_g, ln_b, w_conv_out, b_conv_out, w_attn_out, w_out):
    b, t, _ = h.shape
    proj = h @ w_in
    q = proj[..., :QKV_W].reshape(b, t, N_GROUPS, HEADS_PER_GROUP, HEAD_DIM)
    k = proj[..., QKV_W:2 * QKV_W].reshape(b, t, N_GROUPS, HEADS_PER_GROUP, HEAD_DIM)
    v = proj[..., 2 * QKV_W:3 * QKV_W].reshape(b, t, N_GROUPS, HEADS_PER_GROUP, HEAD_DIM)
    glu = proj[..., 3 * QKV_W:3 * QKV_W + 2 * CONV_W]
    gate_a = jax.nn.sigmoid(proj[..., IN_W - 2 * D_MODEL:IN_W - D_MODEL])
    gate_b = jax.nn.sigmoid(proj[..., IN_W - D_MODEL:])
    outs, lses, new_kv = [], [], []
    for g, (window, dilation) in enumerate(ATTN_GROUPS):
        if kv_bufs is None:
            o, lse = dilated_attn_prompt(q[:, :, g], k[:, :, g], v[:, :, g], window, dilation)
            keep = min(window, t)
            nk = jnp.stack([k[:, -keep:, g], v[:, -keep:, g]], axis=2)
        else:
            o, lse, nk = dilated_attn_sample(q[:, :, g], k[:, :, g], v[:, :, g], kv_bufs[g], window, dilation)
        outs.append(o)
        lses.append(lse)
        new_kv.append(nk)
    wts = jax.nn.softmax(jnp.stack(lses, axis=0), axis=0)
    attn = jnp.sum(wts[..., None] * jnp.stack(outs, axis=0).astype(jnp.float32), axis=0).astype(h.dtype)
    y_a = attn.reshape(b, t, ATTN_OUT_W) @ w_attn_out
    y_b, new_conv = conv_branch(glu, conv_state, dw_kernel, dw_bias, ln_g, ln_b, w_conv_out, b_conv_out)
    y = (gate_a * y_a + gate_b * y_b) @ w_out
    return y, new_kv, new_conv


def peer(h, w_q, sub_keys, u_tab, v_tab):
    b, t, d = h.shape
    n_tok = b * t
    n_blk = -(-n_tok // PEER_BLOCK)
    hf = jnp.pad(h.reshape(n_tok, d), ((0, n_blk * PEER_BLOCK - n_tok), (0, 0))).reshape(n_blk, PEER_BLOCK, d)

    def block(hb):
        qh = (hb @ w_q).reshape(PEER_BLOCK, PEER_HEADS, 2, PK_DIM)
        s = jnp.einsum('thpc,hpkc->thpk', qh, sub_keys, preferred_element_type=jnp.float32)
        s1, i1 = lax.top_k(s[:, :, 0], PEER_TOPK)
        s2, i2 = lax.top_k(s[:, :, 1], PEER_TOPK)
        cand_s = (s1[..., :, None] + s2[..., None, :]).reshape(PEER_BLOCK, PEER_HEADS, PEER_TOPK * PEER_TOPK)
        cand_i = (i1[..., :, None] * N_KEYS + i2[..., None, :]).reshape(PEER_BLOCK, PEER_HEADS, PEER_TOPK * PEER_TOPK)
        top_s, pos = lax.top_k(cand_s, PEER_TOPK)
        expert = jnp.take_along_axis(cand_i, pos, axis=-1)
        gate = jax.nn.softmax(top_s, axis=-1)
        act = jax.nn.gelu(jnp.einsum('thkd,td->thk', u_tab[expert], hb, preferred_element_type=jnp.float32),
                          approximate=False)
        return jnp.einsum('thk,thkd->td', (gate * act).astype(hb.dtype), v_tab[expert])

    out = lax.map(block, hf)
    return out.reshape(n_blk * PEER_BLOCK, d)[:n_tok].reshape(b, t, d)


def setup_inputs(seed: int = 0) -> dict:
    key = jax.random.key(seed)
    ks = jax.random.split(key, 24)
    f32 = jnp.float32

    def nrm(k, shape, scale):
        return jax.random.normal(k, shape, f32) * scale

    return {
        "x_prompt": nrm(ks[0], (BATCH, SEQ, D_MODEL), 1.0),
        "x_sample": nrm(ks[1], (DEC_BATCH, DEC_SEQ, D_MODEL), 1.0),
        "cache_kv_w128": nrm(ks[2], (DEPTH, DEC_BATCH, min(128, PAST_LEN), 2, HEADS_PER_GROUP, HEAD_DIM), 1.0),
        "cache_kv_w512": nrm(ks[3], (DEPTH, DEC_BATCH, min(512, PAST_LEN), 2, HEADS_PER_GROUP, HEAD_DIM), 1.0),
        "cache_kv_w2048": nrm(ks[4], (DEPTH, DEC_BATCH, min(2048, PAST_LEN), 2, HEADS_PER_GROUP, HEAD_DIM), 1.0),
        "state_conv": nrm(ks[5], (DEPTH, DEC_BATCH, CONV_WIDTH - 1, CONV_W), 0.5),
        "norm_mix_g": 1.0 + nrm(ks[6], (DEPTH, D_MODEL), 0.1),
        "norm_ffn_g": 1.0 + nrm(ks[7], (DEPTH, D_MODEL), 0.1),
        "w_in": nrm(ks[8], (DEPTH, D_MODEL, IN_W), D_MODEL ** -0.5),
        "dw_kernel": nrm(ks[9], (DEPTH, CONV_WIDTH, CONV_W), CONV_WIDTH ** -0.5),
        "dw_bias": nrm(ks[10], (DEPTH, CONV_W), 0.02),
        "conv_ln_g": 1.0 + nrm(ks[11], (DEPTH, CONV_W), 0.1),
        "conv_ln_b": nrm(ks[12], (DEPTH, CONV_W), 0.02),
        "w_conv_out": nrm(ks[13], (DEPTH, CONV_W, D_MODEL), CONV_W ** -0.5),
        "b_conv_out": nrm(ks[14], (DEPTH, D_MODEL), 0.02),
        "w_attn_out": nrm(ks[15], (DEPTH, ATTN_OUT_W, D_MODEL), ATTN_OUT_W ** -0.5),
        "w_out": nrm(ks[16], (DEPTH, D_MODEL, D_MODEL), D_MODEL ** -0.5),
        "w_peer_q": nrm(ks[17], (DEPTH, D_MODEL, PEER_Q_W), D_MODEL ** -0.5),
        "peer_sub_keys": nrm(ks[18], (DEPTH, PEER_HEADS, 2, N_KEYS, PK_DIM), PK_DIM ** -0.5),
        "peer_u": nrm(ks[19], (DEPTH, N_EXPERTS, D_MODEL), D_MODEL ** -0.5),
        "peer_v": nrm(ks[20], (DEPTH, N_EXPERTS, D_MODEL), 0.2),
        "norm_final_g": 1.0 + nrm(ks[21], (D_MODEL,), 0.1),
    }


def reference(x_prompt, x_sample, cache_kv_w128, cache_kv_w512, cache_kv_w2048, state_conv,
              norm_mix_g, norm_ffn_g, w_in, dw_kernel, dw_bias, conv_ln_g, conv_ln_b, w_conv_out, b_conv_out,
              w_attn_out, w_out, w_peer_q, peer_sub_keys, peer_u, peer_v, norm_final_g):
    xp, xs = x_prompt, x_sample
    kv_p = [[] for _ in ATTN_GROUPS]
    kv_s = [[] for _ in ATTN_GROUPS]
    conv_p, conv_s = [], []
    for l in range(DEPTH):
        lp = (w_in[l], dw_kernel[l], dw_bias[l], conv_ln_g[l], conv_ln_b[l], w_conv_out[l], b_conv_out[l],
              w_attn_out[l], w_out[l])
        zero_conv = jnp.zeros((xp.shape[0], CONV_WIDTH - 1, CONV_W), xp.dtype)
        yp, nkv, nconv = mixer(rmsnorm(xp, norm_mix_g[l]), None, zero_conv, *lp)
        xp = xp + yp
        xp = xp + peer(rmsnorm(xp, norm_ffn_g[l]), w_peer_q[l], peer_sub_keys[l], peer_u[l], peer_v[l])
        for g in range(N_GROUPS):
            kv_p[g].append(nkv[g])
        conv_p.append(nconv)
        bufs = (cache_kv_w128[l], cache_kv_w512[l], cache_kv_w2048[l])
        ys, nkv, nconv = mixer(rmsnorm(xs, norm_mix_g[l]), bufs, state_conv[l], *lp)
        xs = xs + ys
        xs = xs + peer(rmsnorm(xs, norm_ffn_g[l]), w_peer_q[l], peer_sub_keys[l], peer_u[l], peer_v[l])
        for g in range(N_GROUPS):
            kv_s[g].append(nkv[g])
        conv_s.append(nconv)
    y_prompt = rmsnorm(xp, norm_final_g)
    y_sample = rmsnorm(xs, norm_final_g)
    return (y_prompt, y_sample,
            jnp.stack(kv_p[0]), jnp.stack(kv_p[1]), jnp.stack(kv_p[2]), jnp.stack(conv_p),
            jnp.stack(kv_s[0]), jnp.stack(kv_s[1]), jnp.stack(kv_s[2]), jnp.stack(conv_s))
```

```python
import functools

import jax
import jax.numpy as jnp
from jax import lax
from jax.experimental import pallas as pl
from jax.experimental.pallas import tpu as pltpu

F32 = jnp.float32
BF16 = jnp.bfloat16

HEAD_DIM = 64
HEADS_PER_GROUP = 8
ATTN_GROUPS = ((128, 1), (512, 4), (2048, 16))
N_GROUPS = len(ATTN_GROUPS)
GROUP_W = HEADS_PER_GROUP * HEAD_DIM
QKV_W = N_GROUPS * GROUP_W
ATTN_BLOCK = 128
ATTN_SCALE = HEAD_DIM ** -0.5
CONV_WIDTH = 31
CONV_HIST = 32
N_KEYS = 128
PEER_HEADS = 8
PEER_TOPK = 16
NORM_EPS = 1e-6
NEG_INF = -1e30

VMEM_LIMIT_BYTES = 56 * 1024 * 1024
LANES = 128
HEAD_HALF = 4
HALF_W = HEAD_HALF * HEAD_DIM

NT_DIMS = (((1,), (1,)), ((), ()))


def _params(*sem):
    return pltpu.CompilerParams(dimension_semantics=sem, vmem_limit_bytes=VMEM_LIMIT_BYTES)


def _rmsnorm(x, g):
    return x * lax.rsqrt(jnp.mean(x * x, axis=-1, keepdims=True) + NORM_EPS) * g


def _sigmoid(x):
    return 1.0 / (1.0 + jnp.exp(-x))


def _in_proj_kernel(class_major, tm, x_ref, g_ref, w_ref, *refs):
    if class_major:
        kv_ref, c_ref, ga_ref, gb_ref = refs[:4]
        cm_refs = refs[4:13]
        scr_ref = refs[13]
    else:
        qkv_ref, c_ref, ga_ref, gb_ref = refs[:4]
    h = _rmsnorm(x_ref[...], g_ref[...]).astype(BF16)
    glu_a = None
    for j in range(w_ref.shape[1] // GROUP_W):
        res = jnp.dot(h, w_ref[:, j * GROUP_W:(j + 1) * GROUP_W], preferred_element_type=F32)
        if j < 9:
            kind, g = divmod(j, N_GROUPS)
            if class_major:
                if kind > 0:
                    kv_ref[:, (j - 3) * GROUP_W:(j - 2) * GROUP_W] = res
                val = res * ATTN_SCALE if kind == 0 else res
                d = ATTN_GROUPS[g][1]
                cm = cm_refs[kind * N_GROUPS + g]
                if d == 1:
                    cm[0, 0] = val.astype(BF16)
                else:
                    for t in range(GROUP_W // LANES):
                        scr_ref[t] = val[:, t * LANES:(t + 1) * LANES]
                    for r in range(d):
                        cm[0, r] = jnp.concatenate(
                            [scr_ref[t, pl.ds(r, tm // d, stride=d), :] for t in range(GROUP_W // LANES)], axis=1).astype(BF16)
            else:
                qkv_ref[:, j * GROUP_W:(j + 1) * GROUP_W] = res
        elif j == 9:
            glu_a = res
        elif j == 10:
            c_ref[...] = glu_a * _sigmoid(res)
        elif j < 13:
            ga_ref[:, (j - 11) * GROUP_W:(j - 10) * GROUP_W] = _sigmoid(res)
        else:
            gb_ref[:, (j - 13) * GROUP_W:(j - 12) * GROUP_W] = _sigmoid(res)


def _in_proj(x, g, w_bf16, *, tm, class_major, batch=None, seq=None):
    t, d_model = x.shape
    in_w = w_bf16.shape[1]
    assert t % tm == 0 and in_w == 3 * QKV_W + 2 * GROUP_W + 2 * d_model and d_model == 2 * GROUP_W
    nt = t // tm
    row = lambda i: (i, 0)
    in_specs = [
        pl.BlockSpec((tm, d_model), row),
        pl.BlockSpec((1, d_model), lambda i: (0, 0)),
        pl.BlockSpec((d_model, in_w), lambda i: (0, 0), pipeline_mode=pl.Buffered(1)),
    ]
    common_shapes = [
        jax.ShapeDtypeStruct((t, GROUP_W), F32),
        jax.ShapeDtypeStruct((t, d_model), F32),
        jax.ShapeDtypeStruct((t, d_model), F32),
    ]
    common_specs = [pl.BlockSpec((tm, GROUP_W), row), pl.BlockSpec((tm, d_model), row), pl.BlockSpec((tm, d_model), row)]
    if class_major:
        assert seq % tm == 0
        tps = seq // tm
        out_shape = [jax.ShapeDtypeStruct((t, 2 * QKV_W), F32)] + common_shapes
        out_specs = [pl.BlockSpec((tm, 2 * QKV_W), row)] + common_specs
        for _kind in range(3):
            for (_, d) in ATTN_GROUPS:
                assert tm % (d * 16) == 0
                out_shape.append(jax.ShapeDtypeStruct((batch, d, seq // d, GROUP_W), BF16))
                out_specs.append(pl.BlockSpec((1, d, tm // d, GROUP_W), lambda i, tps=tps: (i // tps, 0, i % tps, 0)))
        scratch = [pltpu.VMEM((GROUP_W // LANES, tm, LANES), F32)]
    else:
        out_shape = [jax.ShapeDtypeStruct((t, 3 * QKV_W), F32)] + common_shapes
        out_specs = [pl.BlockSpec((tm, 3 * QKV_W), row)] + common_specs
        scratch = []
    return pl.pallas_call(
        functools.partial(_in_proj_kernel, class_major, tm),
        grid=(nt,),
        in_specs=in_specs,
        out_specs=out_specs,
        out_shape=out_shape,
        scratch_shapes=scratch,
        compiler_params=_params("parallel"),
        name="in_proj_cm" if class_major else "in_proj_nat",
    )(x, g, w_bf16)


def _attend_heads(qb, kb, vb, mask, n_heads):
    outs, lses = [], []
    for h in range(n_heads):
        sl = slice(h * HEAD_DIM, (h + 1) * HEAD_DIM)
        s = lax.dot_general(qb[:, sl], kb[:, sl], NT_DIMS, preferred_element_type=F32)
        s = jnp.where(mask, s, NEG_INF)
        m = jnp.max(s, axis=-1, keepdims=True)
        p = jnp.exp(s - m)
        den = jnp.sum(p, axis=-1, keepdims=True)
        o = jnp.dot(p.astype(BF16), vb[:, sl], preferred_element_type=F32) / den
        outs.append(o)
        lses.append(jnp.broadcast_to(m + jnp.log(den), o.shape))
    return jnp.concatenate(outs, axis=1), jnp.concatenate(lses, axis=1)


def _attn_prompt_kernel(seq, *refs):
    qkv_refs = refs[:9]
    out_ref = refs[9]
    ocm_ref, lcm_ref, m_ref, n_ref, d_ref = refs[10:]
    n_lt = HALF_W // LANES
    dist = (lax.broadcasted_iota(jnp.int32, (ATTN_BLOCK, 2 * ATTN_BLOCK), 0) + ATTN_BLOCK
            - lax.broadcasted_iota(jnp.int32, (ATTN_BLOCK, 2 * ATTN_BLOCK), 1))
    band_mask = (dist >= 0) & (dist <= ATTN_BLOCK)
    causal_mask = (lax.broadcasted_iota(jnp.int32, (ATTN_BLOCK, ATTN_BLOCK), 0)
                   >= lax.broadcasted_iota(jnp.int32, (ATTN_BLOCK, ATTN_BLOCK), 1))

    def put(ref, rows, val):
        for t in range(n_lt):
            ref[t, rows, :] = val[:, t * LANES:(t + 1) * LANES]

    for g, (window, d) in enumerate(ATTN_GROUPS):
        assert window // d == ATTN_BLOCK
        q_ref, k_ref, v_ref = qkv_refs[g], qkv_refs[N_GROUPS + g], qkv_refs[2 * N_GROUPS + g]
        sd = seq // d
        nb = sd // ATTN_BLOCK

        def class_body(r, carry, q_ref=q_ref, k_ref=k_ref, v_ref=v_ref, sd=sd, nb=nb):
            base = pl.multiple_of(r * sd, ATTN_BLOCK)
            o, l = _attend_heads(q_ref[0, r, 0:ATTN_BLOCK, :], k_ref[0, r, 0:ATTN_BLOCK, :],
                                 v_ref[0, r, 0:ATTN_BLOCK, :], causal_mask, HEAD_HALF)
            put(ocm_ref, pl.ds(base, ATTN_BLOCK), o)
            put(lcm_ref, pl.ds(base, ATTN_BLOCK), l)

            def block_body(i, c2):
                q0 = pl.multiple_of(i * ATTN_BLOCK, ATTN_BLOCK)
                k0 = pl.multiple_of((i - 1) * ATTN_BLOCK, ATTN_BLOCK)
                o, l = _attend_heads(q_ref[0, r, pl.ds(q0, ATTN_BLOCK), :], k_ref[0, r, pl.ds(k0, 2 * ATTN_BLOCK), :],
                                     v_ref[0, r, pl.ds(k0, 2 * ATTN_BLOCK), :], band_mask, HEAD_HALF)
                dst = pl.multiple_of(base + q0, ATTN_BLOCK)
                put(ocm_ref, pl.ds(dst, ATTN_BLOCK), o)
                put(lcm_ref, pl.ds(dst, ATTN_BLOCK), l)
                return c2

            if nb > 1:
                lax.fori_loop(1, nb, block_body, 0)
            return carry

        lax.fori_loop(0, d, class_body, 0)

        if g == 0:
            assert d == 1
            m_ref[...] = lcm_ref[...]
            n_ref[...] = ocm_ref[...]
            d_ref[...] = jnp.ones(d_ref.shape, F32)
        else:
            for r in range(d):
                rows = pl.ds(r, sd, stride=d)
                for t in range(n_lt):
                    o_r = ocm_ref[t, r * sd:(r + 1) * sd, :]
                    l_r = lcm_ref[t, r * sd:(r + 1) * sd, :]
                    m_old = m_ref[t, rows, :]
                    m_new = jnp.maximum(m_old, l_r)
                    a = jnp.exp(m_old - m_new)
                    b = jnp.exp(l_r - m_new)
                    n_ref[t, rows, :] = n_ref[t, rows, :] * a + o_r * b
                    d_ref[t, rows, :] = d_ref[t, rows, :] * a + b
                    m_ref[t, rows, :] = m_new
    for t in range(n_lt):
        out_ref[0, :, t * LANES:(t + 1) * LANES] = (n_ref[t] / d_ref[t]).astype(BF16)


def _attn_prompt(cm, batch, seq):
    in_specs = []
    for _kind in range(3):
        for (_, d) in ATTN_GROUPS:
            assert seq % (d * ATTN_BLOCK) == 0
            in_specs.append(pl.BlockSpec((1, d, seq // d, HALF_W), lambda b, hh: (b, 0, 0, hh)))
    return pl.pallas_call(
        functools.partial(_attn_prompt_kernel, seq),
        grid=(batch, GROUP_W // HALF_W),
        in_specs=in_specs,
        out_specs=pl.BlockSpec((1, seq, HALF_W), lambda b, hh: (b, 0, hh)),
        out_shape=jax.ShapeDtypeStruct((batch, seq, GROUP_W), BF16),
        scratch_shapes=[pltpu.VMEM((HALF_W // LANES, seq, LANES), F32) for _ in range(5)],
        compiler_params=_params("parallel", "parallel"),
        name="attn_prompt",
    )(*cm)


def _attn_sample_kernel(ns, qkv_ref, c0_ref, c1_ref, c2_ref, out_ref):
    cache_refs = (c0_ref, c1_ref, c2_ref)
    qkv = qkv_ref[...]
    lane = lax.broadcasted_iota(jnp.int32, (HEADS_PER_GROUP, GROUP_W), 1)
    hrow = lax.broadcasted_iota(jnp.int32, (HEADS_PER_GROUP, GROUP_W), 0)
    head_mask = (lane // HEAD_DIM == hrow).astype(F32)
    outs, lses = [], []
    for g, (window, d) in enumerate(ATTN_GROUPS):
        span = window // d
        assert span == ATTN_BLOCK and cache_refs[g].shape[2:4] == (min(d, ns), span)
        q = qkv[:, g * GROUP_W:(g + 1) * GROUP_W] * ATTN_SCALE
        k_new = qkv[:, QKV_W + g * GROUP_W:QKV_W + (g + 1) * GROUP_W]
        v_new = qkv[:, 2 * QKV_W + g * GROUP_W:2 * QKV_W + (g + 1) * GROUP_W]
        o_rows = [None] * ns
        l_rows = [None] * ns
        for r in range(min(d, ns)):
            toks = list(range(r, ns, d))
            nq = len(toks)
            pad = (-nq) % 8
            cls = cache_refs[g][0, 0, r]
            parts_k = [cls[:, :GROUP_W]] + [k_new[n:n + 1] for n in toks]
            parts_v = [cls[:, GROUP_W:]] + [v_new[n:n + 1] for n in toks]
            if pad:
                parts_k.append(jnp.zeros((pad, GROUP_W), F32))
                parts_v.append(jnp.zeros((pad, GROUP_W), F32))
            k_cls = jnp.concatenate(parts_k, axis=0).astype(BF16)
            v_cls = jnp.concatenate(parts_v, axis=0).astype(BF16)
            nk = span + nq + pad
            q_exp = jnp.concatenate([q[n:n + 1] * head_mask for n in toks], axis=0).astype(BF16)
            s = lax.dot_general(q_exp, k_cls, NT_DIMS, preferred_element_type=F32)
            rq = lax.broadcasted_iota(jnp.int32, (nq * HEADS_PER_GROUP, nk), 0) // HEADS_PER_GROUP
            kp = lax.broadcasted_iota(jnp.int32, (nq * HEADS_PER_GROUP, nk), 1)
            dist = span + rq - kp
            s = jnp.where((dist >= 0) & (dist <= span), s, NEG_INF)
            m = jnp.max(s, axis=-1, keepdims=True)
            p = jnp.exp(s - m)
            den = jnp.sum(p, axis=-1, keepdims=True)
            pv = jnp.dot(p.astype(BF16), v_cls, preferred_element_type=F32) / den
            lse = jnp.broadcast_to(m + jnp.log(den), pv.shape)
            for qi_, n in enumerate(toks):
                blk = slice(qi_ * HEADS_PER_GROUP, (qi_ + 1) * HEADS_PER_GROUP)
                o_rows[n] = jnp.sum(pv[blk] * head_mask, axis=0, keepdims=True)
                l_rows[n] = jnp.sum(lse[blk] * head_mask, axis=0, keepdims=True)
        outs.append(jnp.concatenate(o_rows, axis=0))
        lses.append(jnp.concatenate(l_rows, axis=0))
    m = jnp.maximum(jnp.maximum(lses[0], lses[1]), lses[2])
    es = [jnp.exp(l - m) for l in lses]
    num = es[0] * outs[0] + es[1] * outs[1] + es[2] * outs[2]
    out_ref[...] = num / (es[0] + es[1] + es[2])


def _attn_sample(qkv, caches, layer, dec_batch, ns):
    assert ns % 8 == 0
    in_specs = [pl.BlockSpec((ns, 3 * QKV_W), lambda b: (b, 0))]
    for c in caches:
        in_specs.append(pl.BlockSpec((1, 1) + c.shape[2:], lambda b, layer=layer: (layer, b, 0, 0, 0)))
    return pl.pallas_call(
        functools.partial(_attn_sample_kernel, ns),
        grid=(dec_batch,),
        in_specs=in_specs,
        out_specs=pl.BlockSpec((ns, GROUP_W), lambda b: (b, 0)),
        out_shape=jax.ShapeDtypeStruct((dec_batch * ns, GROUP_W), F32),
        compiler_params=_params("parallel"),
        name="attn_sample",
    )(qkv, *caches)


def _conv_kernel(tc, has_hist, *refs):
    if has_hist:
        c_ref, hist_ref, dw_ref, bias_ref, lng_ref, lnb_ref, out_ref, ext_ref = refs
        ext_ref[0:CONV_HIST, :] = hist_ref[0]
        ext_ref[CONV_HIST:, :] = c_ref[...]
    else:
        c_ref, dw_ref, bias_ref, lng_ref, lnb_ref, out_ref, ext_ref = refs
        i = pl.program_id(1)
        t0 = pl.multiple_of(i * tc, 8)
        prev = pl.multiple_of(jnp.maximum(t0 - CONV_HIST, 0), 8)
        hist = c_ref[pl.ds(prev, CONV_HIST), :]
        ext_ref[0:CONV_HIST, :] = jnp.where(i > 0, hist, 0.0)
        ext_ref[CONV_HIST:, :] = c_ref[pl.ds(t0, tc), :]
    off = CONV_HIST - (CONV_WIDTH - 1)
    acc = ext_ref[pl.ds(off, tc), :] * dw_ref[0:1, :]
    for k in range(1, CONV_WIDTH):
        acc = acc + ext_ref[pl.ds(off + k, tc), :] * dw_ref[k:k + 1, :]
    y = acc + bias_ref[...]
    mu = jnp.mean(y, axis=-1, keepdims=True)
    yc = y - mu
    var = jnp.mean(yc * yc, axis=-1, keepdims=True)
    y = yc * lax.rsqrt(var + NORM_EPS) * lng_ref[...] + lnb_ref[...]
    out_ref[...] = y * _sigmoid(y)


def _conv_prompt(c, dw, bias, lng, lnb, batch, seq, tc):
    assert seq % tc == 0 and tc % 8 == 0
    nt = seq // tc
    vec = lambda n: pl.BlockSpec((1, n), lambda b, i: (0, 0))
    return pl.pallas_call(
        functools.partial(_conv_kernel, tc, False),
        grid=(batch, nt),
        in_specs=[pl.BlockSpec((seq, GROUP_W), lambda b, i: (b, 0)),
                  pl.BlockSpec((CONV_WIDTH, GROUP_W), lambda b, i: (0, 0)), vec(GROUP_W), vec(GROUP_W), vec(GROUP_W)],
        out_specs=pl.BlockSpec((tc, GROUP_W), lambda b, i, nt=nt: (b * nt + i, 0)),
        out_shape=jax.ShapeDtypeStruct((batch * seq, GROUP_W), F32),
        scratch_shapes=[pltpu.VMEM((tc + CONV_HIST, GROUP_W), F32)],
        compiler_params=_params("parallel", "arbitrary"),
        name="conv_prompt",
    )(c, dw, bias, lng, lnb)


def _conv_sample(c, hist, dw, bias, lng, lnb, dec_batch, ns):
    vec = lambda n: pl.BlockSpec((1, n), lambda b: (0, 0))
    return pl.pallas_call(
        functools.partial(_conv_kernel, ns, True),
        grid=(dec_batch,),
        in_specs=[pl.BlockSpec((ns, GROUP_W), lambda b: (b, 0)),
                  pl.BlockSpec((1, CONV_HIST, GROUP_W), lambda b: (b, 0, 0)),
                  pl.BlockSpec((CONV_WIDTH, GROUP_W), lambda b: (0, 0)), vec(GROUP_W), vec(GROUP_W), vec(GROUP_W)],
        out_specs=pl.BlockSpec((ns, GROUP_W), lambda b: (b, 0)),
        out_shape=jax.ShapeDtypeStruct((dec_batch * ns, GROUP_W), F32),
        scratch_shapes=[pltpu.VMEM((ns + CONV_HIST, GROUP_W), F32)],
        compiler_params=_params("parallel"),
        name="conv_sample",
    )(c, hist, dw, bias, lng, lnb)


def _merge_kernel(x_ref, attn_ref, u_ref, ga_ref, gb_ref, wa_ref, wc_ref, bc_ref, wo_ref, out_ref):
    y_a = jnp.dot(attn_ref[...].astype(BF16), wa_ref[...], preferred_element_type=F32)
    y_b = jnp.dot(u_ref[...].astype(BF16), wc_ref[...], preferred_element_type=F32) + bc_ref[...]
    y = (ga_ref[...] * y_a + gb_ref[...] * y_b).astype(BF16)
    out_ref[...] = x_ref[...] + jnp.dot(y, wo_ref[...], preferred_element_type=F32)


def _merge(x, attn, u, ga, gb, wa, wc, bc, wo, *, tm):
    t, d_model = x.shape
    assert t % tm == 0
    row = lambda i: (i, 0)
    const = lambda i: (0, 0)
    return pl.pallas_call(
        _merge_kernel,
        grid=(t // tm,),
        in_specs=[pl.BlockSpec((tm, d_model), row), pl.BlockSpec((tm, GROUP_W), row), pl.BlockSpec((tm, GROUP_W), row),
                  pl.BlockSpec((tm, d_model), row), pl.BlockSpec((tm, d_model), row),
                  pl.BlockSpec((GROUP_W, d_model), const), pl.BlockSpec((GROUP_W, d_model), const),
                  pl.BlockSpec((1, d_model), const), pl.BlockSpec((d_model, d_model), const)],
        out_specs=pl.BlockSpec((tm, d_model), row),
        out_shape=jax.ShapeDtypeStruct((t, d_model), F32),
        compiler_params=_params("parallel"),
        name="merge",
    )(x, attn, u, ga, gb, wa, wc, bc, wo)


def _top_values(s, out_ref, n):
    cur = s
    for j in range(n):
        m = jnp.max(cur, axis=0, keepdims=True)
        out_ref[j:j + 1, :] = m
        if j + 1 < n:
            cur = jnp.where(cur == m, -jnp.inf, cur)


_CAND = [(i, PEER_TOPK // (i + 1)) for i in range(PEER_TOPK)]
_CAND_ROWS = 64


def _peer_kernel(tp, ech, lb, x_ref, g_ref, wq_ref, keys_ref, u_ref, vt_ref, out_ref,
                 ht_ref, qt_ref, s1_ref, e1_ref, s2_ref, e2_ref, tau_ref, at_ref, gt_ref, acc_ref, a_ref, b_ref, cand_ref, top_ref):
    c = pl.program_id(1)
    n_lb = tp // lb

    @pl.when(c == 0)
    def _route():
        h2 = _rmsnorm(x_ref[...], g_ref[...])
        ht = h2.T
        ht_ref[...] = ht.astype(BF16)
        qt = jnp.dot(wq_ref[...], ht, preferred_element_type=F32, precision=lax.Precision.HIGHEST)
        qt_ref[...] = qt.reshape(qt_ref.shape)
        acc_ref[...] = jnp.zeros(acc_ref.shape, F32)

        def head_body(h, carry):
            s1_all = jnp.dot(keys_ref[2 * h], qt_ref[2 * h], preferred_element_type=F32, precision=lax.Precision.HIGHEST)
            s2_all = jnp.dot(keys_ref[2 * h + 1], qt_ref[2 * h + 1], preferred_element_type=F32, precision=lax.Precision.HIGHEST)
            s1_ref[h] = s1_all
            s2_ref[h] = s2_all
            for j in range(n_lb):
                cols = slice(j * lb, (j + 1) * lb)
                s1 = s1_ref[h, :, cols]
                s2 = s2_ref[h, :, cols]
                _top_values(s1, a_ref, PEER_TOPK)
                _top_values(s2, b_ref, PEER_TOPK)
                cand_ref[...] = jnp.full(cand_ref.shape, -jnp.inf, F32)
                row = 0
                for (i, cnt) in _CAND:
                    cand_ref[row:row + cnt, :] = a_ref[i:i + 1, :] + b_ref[0:cnt, :]
                    row += cnt
                _top_values(cand_ref[...], top_ref, PEER_TOPK)
                top = top_ref[...]
                tau = top[PEER_TOPK - 1:PEER_TOPK, :]
                z = jnp.sum(jnp.exp(top - top[0:1, :]), axis=0, keepdims=True)
                a0, a15 = a_ref[0:1, :], a_ref[PEER_TOPK - 1:PEER_TOPK, :]
                b0, b15 = b_ref[0:1, :], b_ref[PEER_TOPK - 1:PEER_TOPK, :]
                tau_ref[h, :, cols] = jnp.broadcast_to(tau, (8, lb))
                e1_ref[h, :, cols] = jnp.where(s1 >= a15, jnp.exp(s1 - a0), 0.0)
                e2_ref[h, :, cols] = jnp.where(s2 >= b15, jnp.exp(s2 - b0) / z, 0.0)
            return carry

        lax.fori_loop(0, PEER_HEADS, head_body, 0)

    at_ref[...] = jnp.dot(u_ref[...], ht_ref[...], preferred_element_type=F32)
    i1_rows = pl.ds(pl.multiple_of(c * (ech // N_KEYS), 8), ech // N_KEYS)
    for i1l in range(ech // N_KEYS):
        rows = slice(i1l * N_KEYS, (i1l + 1) * N_KEYS)
        for j in range(n_lb):
            cols = slice(j * lb, (j + 1) * lb)
            w = jnp.zeros((N_KEYS, lb), F32)
            for h in range(PEER_HEADS):
                s1_row = s1_ref[h, i1_rows, cols][i1l:i1l + 1]
                e1_row = e1_ref[h, i1_rows, cols][i1l:i1l + 1]
                keep = s2_ref[h, :, cols] + s1_row >= tau_ref[h, 0:1, cols]
                w = w + jnp.where(keep, e2_ref[h, :, cols] * e1_row, 0.0)
            a = at_ref[rows, cols]
            act = 0.5 * a * (1.0 + lax.erf(a * (0.5 ** 0.5)))
            gt_ref[rows, cols] = (w * act).astype(BF16)
    acc_ref[...] += jnp.dot(vt_ref[...], gt_ref[...], preferred_element_type=F32)

    @pl.when(c == pl.num_programs(1) - 1)
    def _finish():
        out_ref[...] = x_ref[...] + acc_ref[...].T


def _peer(x, g, wq_t, keys, u_bf16, vt_bf16, *, tp, ech=8 * N_KEYS, lb=LANES):
    t, d_model = x.shape
    n_exp = u_bf16.shape[0]
    assert t % tp == 0 and n_exp % ech == 0 and ech == 8 * N_KEYS and tp % lb == 0 and n_exp == N_KEYS * N_KEYS
    q_w = wq_t.shape[0]
    assert q_w == PEER_HEADS * 2 * N_KEYS
    meta = lambda: pltpu.VMEM((PEER_HEADS, N_KEYS, tp), F32)
    return pl.pallas_call(
        functools.partial(_peer_kernel, tp, ech, lb),
        grid=(t // tp, n_exp // ech),
        in_specs=[pl.BlockSpec((tp, d_model), lambda i, c: (i, 0)),
                  pl.BlockSpec((1, d_model), lambda i, c: (0, 0)),
                  pl.BlockSpec((q_w, d_model), lambda i, c: (0, 0), pipeline_mode=pl.Buffered(1)),
                  pl.BlockSpec((2 * PEER_HEADS, N_KEYS, N_KEYS), lambda i, c: (0, 0, 0), pipeline_mode=pl.Buffered(1)),
                  pl.BlockSpec((ech, d_model), lambda i, c: (c, 0)),
                  pl.BlockSpec((d_model, ech), lambda i, c: (0, c))],
        out_specs=pl.BlockSpec((tp, d_model), lambda i, c: (i, 0)),
        out_shape=jax.ShapeDtypeStruct((t, d_model), F32),
        scratch_shapes=[pltpu.VMEM((d_model, tp), BF16),
                        pltpu.VMEM((2 * PEER_HEADS, N_KEYS, tp), F32),
                        meta(), meta(), meta(), meta(),
                        pltpu.VMEM((PEER_HEADS, 8, tp), F32),
                        pltpu.VMEM((ech, tp), F32), pltpu.VMEM((ech, tp), BF16),
                        pltpu.VMEM((d_model, tp), F32),
                        pltpu.VMEM((PEER_TOPK, lb), F32), pltpu.VMEM((PEER_TOPK, lb), F32),
                        pltpu.VMEM((_CAND_ROWS, lb), F32), pltpu.VMEM((PEER_TOPK, lb), F32)],
        compiler_params=_params("parallel", "arbitrary"),
        name="peer",
    )(x, g, wq_t, keys, u_bf16, vt_bf16)


def _final_norm_kernel(x_ref, g_ref, out_ref):
    out_ref[...] = _rmsnorm(x_ref[...], g_ref[...])


def _final_norm(x, g, *, tm):
    t, d_model = x.shape
    return pl.pallas_call(
        _final_norm_kernel,
        grid=(t // tm,),
        in_specs=[pl.BlockSpec((tm, d_model), lambda i: (i, 0)), pl.BlockSpec((1, d_model), lambda i: (0, 0))],
        out_specs=pl.BlockSpec((tm, d_model), lambda i: (i, 0)),
        out_shape=jax.ShapeDtypeStruct((t, d_model), F32),
        compiler_params=_params("parallel"),
        name="final_norm",
    )(x, g)


def _kv_tail(kv, g, batch, seq, keep):
    kv = kv.reshape(batch, seq, 2, N_GROUPS, HEADS_PER_GROUP, HEAD_DIM)
    return kv[:, seq - keep:, :, g]


def kernel(x_prompt, x_sample, cache_kv_w128, cache_kv_w512, cache_kv_w2048, state_conv, norm_mix_g, norm_ffn_g, w_in, dw_kernel, dw_bias, conv_ln_g, conv_ln_b, w_conv_out, b_conv_out, w_attn_out, w_out, w_peer_q, peer_sub_keys, peer_u, peer_v, norm_final_g):
    batch, seq, d_model = x_prompt.shape
    dec_batch, ns, _ = x_sample.shape
    depth = w_in.shape[0]
    tp_tokens, ts_tokens = batch * seq, dec_batch * ns
    tm_p = 512
    caches = []
    for c, (window, d) in zip((cache_kv_w128, cache_kv_w512, cache_kv_w2048), ATTN_GROUPS):
        assert c.shape[2] == window
        c = c.reshape(c.shape[0], c.shape[1], window // d, d, 2 * GROUP_W)[:, :, :, :min(d, ns)]
        caches.append(c.transpose(0, 1, 3, 2, 4))
    assert state_conv.shape[2] == CONV_WIDTH - 1
    hist_pad = CONV_HIST - (CONV_WIDTH - 1)

    xp = x_prompt.reshape(tp_tokens, d_model)
    xs = x_sample.reshape(ts_tokens, d_model)
    kv_p = [[] for _ in ATTN_GROUPS]
    kv_s = [[] for _ in ATTN_GROUPS]
    conv_p, conv_s = [], []
    vec = lambda a: a.reshape(1, -1)
    for l in range(depth):
        w_in_b = w_in[l].astype(BF16)
        wa_b, wc_b, wo_b = w_attn_out[l].astype(BF16), w_conv_out[l].astype(BF16), w_out[l].astype(BF16)
        wq_t = w_peer_q[l].T
        keys = peer_sub_keys[l].reshape(2 * PEER_HEADS, N_KEYS, -1)
        u_b = peer_u[l].astype(BF16)
        vt_b = peer_v[l].T.astype(BF16)
        conv_w = (dw_kernel[l], vec(dw_bias[l]), vec(conv_ln_g[l]), vec(conv_ln_b[l]))
        mix_g, ffn_g = vec(norm_mix_g[l]), vec(norm_ffn_g[l])

        outs = _in_proj(xp, mix_g, w_in_b, tm=tm_p, class_major=True, batch=batch, seq=seq)
        kv, c, ga, gb = outs[:4]
        attn = _attn_prompt(outs[4:], batch, seq).reshape(tp_tokens, GROUP_W)
        u = _conv_prompt(c, *conv_w, batch, seq, 512)
        xp = _merge(xp, attn, u, ga, gb, wa_b, wc_b, vec(b_conv_out[l]), wo_b, tm=tm_p)
        xp = _peer(xp, ffn_g, wq_t, keys, u_b, vt_b, tp=512)
        for g, (window, _) in enumerate(ATTN_GROUPS):
            kv_p[g].append(_kv_tail(kv, g, batch, seq, min(window, seq)))
        conv_p.append(c.reshape(batch, seq, GROUP_W)[:, seq - (CONV_WIDTH - 1):])

        qkv, c, ga, gb = _in_proj(xs, mix_g, w_in_b, tm=ts_tokens, class_major=False)
        attn = _attn_sample(qkv, caches, l, dec_batch, ns)
        hist = jnp.pad(state_conv[l], ((0, 0), (hist_pad, 0), (0, 0)))
        u = _conv_sample(c, hist, *conv_w, dec_batch, ns)
        xs = _merge(xs, attn, u, ga, gb, wa_b, wc_b, vec(b_conv_out[l]), wo_b, tm=ts_tokens)
        xs = _peer(xs, ffn_g, wq_t, keys, u_b, vt_b, tp=ts_tokens)
        kv_new = qkv[:, QKV_W:].reshape(dec_batch, ns, 2, N_GROUPS, HEADS_PER_GROUP, HEAD_DIM)
        for g, cache in enumerate((cache_kv_w128, cache_kv_w512, cache_kv_w2048)):
            full = jnp.concatenate([cache[l], kv_new[:, :, :, g]], axis=1)
            keep = min(ATTN_GROUPS[g][0], full.shape[1])
            kv_s[g].append(full[:, full.shape[1] - keep:])
        full = jnp.concatenate([state_conv[l], c.reshape(dec_batch, ns, GROUP_W)], axis=1)
        conv_s.append(full[:, full.shape[1] - (CONV_WIDTH - 1):])

    y_prompt = _final_norm(xp, vec(norm_final_g), tm=tm_p).reshape(batch, seq, d_model)
    y_sample = _final_norm(xs, vec(norm_final_g), tm=ts_tokens).reshape(dec_batch, ns, d_model)
    return (y_prompt, y_sample,
            jnp.stack(kv_p[0]), jnp.stack(kv_p[1]), jnp.stack(kv_p[2]), jnp.stack(conv_p),
            jnp.stack(kv_s[0]), jnp.stack(kv_s[1]), jnp.stack(kv_s[2]), jnp.stack(conv_s))
```

```python
import functools

import jax
import jax.numpy as jnp
from jax import lax
from jax.experimental import pallas as pl
from jax.experimental.pallas import tpu as pltpu

F32 = jnp.float32
BF16 = jnp.bfloat16

HEAD_DIM = 64
HEADS_PER_GROUP = 8
ATTN_GROUPS = ((128, 1), (512, 4), (2048, 16))
N_GROUPS = len(ATTN_GROUPS)
GROUP_W = HEADS_PER_GROUP * HEAD_DIM
QKV_W = N_GROUPS * GROUP_W
ATTN_BLOCK = 128
ATTN_SCALE = HEAD_DIM ** -0.5
CONV_WIDTH = 31
CONV_HIST = 32
N_KEYS = 128
PEER_HEADS = 8
PEER_TOPK = 16
NORM_EPS = 1e-6
NEG_INF = -1e30

VMEM_LIMIT_BYTES = 56 * 1024 * 1024
LANES = 128
HEAD_HALF = 4
HALF_W = HEAD_HALF * HEAD_DIM

NT_DIMS = (((1,), (1,)), ((), ()))


def _params(*sem):
    return pltpu.CompilerParams(dimension_semantics=sem, vmem_limit_bytes=VMEM_LIMIT_BYTES)


def _rmsnorm(x, g):
    return x * lax.rsqrt(jnp.mean(x * x, axis=-1, keepdims=True) + NORM_EPS) * g


def _sigmoid(x):
    return 1.0 / (1.0 + jnp.exp(-x))


def _in_proj_kernel(class_major, tm, seq, x_ref, g_ref, w_ref, *refs):
    if class_major:
        c_ref, ga_ref, gb_ref = refs[3:6]
        cm_refs = refs[6:15]
        tail_refs = refs[15:18]
        scr_ref = refs[18]
        tps = seq // tm
        last_tile = pl.program_id(0) % tps == tps - 1
    else:
        qkv_ref, c_ref, ga_ref, gb_ref = refs[:4]
    h = _rmsnorm(x_ref[...], g_ref[...]).astype(BF16)
    glu_a = None
    for j in range(w_ref.shape[1] // GROUP_W):
        res = jnp.dot(h, w_ref[:, j * GROUP_W:(j + 1) * GROUP_W], preferred_element_type=F32)
        if j < 9:
            kind, g = divmod(j, N_GROUPS)
            if class_major:
                if kind > 0:
                    keep = tail_refs[g].shape[-1]
                    if min(ATTN_GROUPS[g][0], seq) == seq:
                        tail_refs[g][0, 0, kind - 1] = res.T
                    else:
                        @pl.when(last_tile)
                        def _(res=res, g=g, kind=kind, keep=keep):
                            tail_refs[g][0, 0, kind - 1] = res[tm - keep:, :].T
                val = res * ATTN_SCALE if kind == 0 else res
                d = ATTN_GROUPS[g][1]
                cm = cm_refs[kind * N_GROUPS + g]
                if d == 1:
                    cm[0, 0] = val.astype(BF16)
                else:
                    for t in range(GROUP_W // LANES):
                        scr_ref[t] = val[:, t * LANES:(t + 1) * LANES]
                    for r in range(d):
                        cm[0, r] = jnp.concatenate(
                            [scr_ref[t, pl.ds(r, tm // d, stride=d), :] for t in range(GROUP_W // LANES)], axis=1).astype(BF16)
            else:
                qkv_ref[:, j * GROUP_W:(j + 1) * GROUP_W] = res
        elif j == 9:
            glu_a = res
        elif j == 10:
            c_ref[...] = glu_a * _sigmoid(res)
        elif j < 13:
            ga_ref[:, (j - 11) * GROUP_W:(j - 10) * GROUP_W] = _sigmoid(res)
        else:
            gb_ref[:, (j - 13) * GROUP_W:(j - 12) * GROUP_W] = _sigmoid(res)


def _in_proj(x, g, w_bf16, *, tm, class_major, batch=None, seq=None, layer=None, tails=None):
    t, d_model = x.shape
    in_w = w_bf16.shape[1]
    assert t % tm == 0 and in_w == 3 * QKV_W + 2 * GROUP_W + 2 * d_model and d_model == 2 * GROUP_W
    nt = t // tm
    row = lambda i: (i, 0)
    in_specs = [
        pl.BlockSpec((tm, d_model), row),
        pl.BlockSpec((1, d_model), lambda i: (0, 0)),
        pl.BlockSpec((d_model, in_w), lambda i: (0, 0), pipeline_mode=pl.Buffered(1)),
    ]
    common_shapes = [
        jax.ShapeDtypeStruct((t, GROUP_W), F32),
        jax.ShapeDtypeStruct((t, d_model), F32),
        jax.ShapeDtypeStruct((t, d_model), F32),
    ]
    common_specs = [pl.BlockSpec((tm, GROUP_W), row), pl.BlockSpec((tm, d_model), row), pl.BlockSpec((tm, d_model), row)]
    aliases = {}
    args = (x, g, w_bf16)
    if class_major:
        assert seq % tm == 0
        tps = seq // tm
        in_specs += [pl.BlockSpec(memory_space=pl.ANY)] * N_GROUPS
        args += tuple(tails)
        out_shape = list(common_shapes)
        out_specs = list(common_specs)
        for _kind in range(3):
            for (_, d) in ATTN_GROUPS:
                assert tm % (d * 16) == 0
                out_shape.append(jax.ShapeDtypeStruct((batch, d, seq // d, GROUP_W), BF16))
                out_specs.append(pl.BlockSpec((1, d, tm // d, GROUP_W), lambda i, tps=tps: (i // tps, 0, i % tps, 0)))
        for g_idx, ((window, _), tail) in enumerate(zip(ATTN_GROUPS, tails)):
            keep = min(window, seq)
            assert tail.shape[1:] == (batch, 2, GROUP_W, keep) and (keep == seq or (keep <= tm and keep % LANES == 0))
            aliases[3 + g_idx] = len(out_shape)
            out_shape.append(jax.ShapeDtypeStruct(tail.shape, F32))
            if keep == seq:
                out_specs.append(pl.BlockSpec((1, 1, 2, GROUP_W, tm), lambda i, tps=tps: (layer, i // tps, 0, 0, i % tps)))
            else:
                out_specs.append(pl.BlockSpec((1, 1, 2, GROUP_W, keep), lambda i, tps=tps: (layer, i // tps, 0, 0, 0)))
        scratch = [pltpu.VMEM((GROUP_W // LANES, tm, LANES), F32)]
    else:
        out_shape = [jax.ShapeDtypeStruct((t, 3 * QKV_W), F32)] + common_shapes
        out_specs = [pl.BlockSpec((tm, 3 * QKV_W), row)] + common_specs
        scratch = []
    return pl.pallas_call(
        functools.partial(_in_proj_kernel, class_major, tm, seq),
        grid=(nt,),
        in_specs=in_specs,
        out_specs=out_specs,
        out_shape=out_shape,
        scratch_shapes=scratch,
        input_output_aliases=aliases,
        compiler_params=_params("arbitrary"),
        name="in_proj_cm" if class_major else "in_proj_nat",
    )(*args)


def _attend_heads(qb, kb, vb, mask, n_heads):
    outs, lses = [], []
    for h in range(n_heads):
        sl = slice(h * HEAD_DIM, (h + 1) * HEAD_DIM)
        s = lax.dot_general(qb[:, sl], kb[:, sl], NT_DIMS, preferred_element_type=F32)
        s = jnp.where(mask, s, NEG_INF)
        m = jnp.max(s, axis=-1, keepdims=True)
        p = jnp.exp(s - m)
        den = jnp.sum(p, axis=-1, keepdims=True)
        o = jnp.dot(p.astype(BF16), vb[:, sl], preferred_element_type=F32) / den
        outs.append(o)
        lses.append(jnp.broadcast_to(m + jnp.log(den), o.shape))
    return jnp.concatenate(outs, axis=1), jnp.concatenate(lses, axis=1)


def _attn_prompt_kernel(seq, *refs):
    qkv_refs = refs[:9]
    out_ref = refs[9]
    ocm_ref, lcm_ref, m_ref, n_ref, d_ref = refs[10:]
    n_lt = HALF_W // LANES
    dist = (lax.broadcasted_iota(jnp.int32, (ATTN_BLOCK, 2 * ATTN_BLOCK), 0) + ATTN_BLOCK
            - lax.broadcasted_iota(jnp.int32, (ATTN_BLOCK, 2 * ATTN_BLOCK), 1))
    band_mask = (dist >= 0) & (dist <= ATTN_BLOCK)
    causal_mask = (lax.broadcasted_iota(jnp.int32, (ATTN_BLOCK, ATTN_BLOCK), 0)
                   >= lax.broadcasted_iota(jnp.int32, (ATTN_BLOCK, ATTN_BLOCK), 1))

    def put(ref, rows, val):
        for t in range(n_lt):
            ref[t, rows, :] = val[:, t * LANES:(t + 1) * LANES]

    for g, (window, d) in enumerate(ATTN_GROUPS):
        assert window // d == ATTN_BLOCK
        q_ref, k_ref, v_ref = qkv_refs[g], qkv_refs[N_GROUPS + g], qkv_refs[2 * N_GROUPS + g]
        sd = seq // d
        nb = sd // ATTN_BLOCK

        def class_body(r, carry, q_ref=q_ref, k_ref=k_ref, v_ref=v_ref, sd=sd, nb=nb):
            base = pl.multiple_of(r * sd, ATTN_BLOCK)
            o, l = _attend_heads(q_ref[0, r, 0:ATTN_BLOCK, :], k_ref[0, r, 0:ATTN_BLOCK, :],
                                 v_ref[0, r, 0:ATTN_BLOCK, :], causal_mask, HEAD_HALF)
            put(ocm_ref, pl.ds(base, ATTN_BLOCK), o)
            put(lcm_ref, pl.ds(base, ATTN_BLOCK), l)

            def block_body(i, c2):
                q0 = pl.multiple_of(i * ATTN_BLOCK, ATTN_BLOCK)
                k0 = pl.multiple_of((i - 1) * ATTN_BLOCK, ATTN_BLOCK)
                o, l = _attend_heads(q_ref[0, r, pl.ds(q0, ATTN_BLOCK), :], k_ref[0, r, pl.ds(k0, 2 * ATTN_BLOCK), :],
                                     v_ref[0, r, pl.ds(k0, 2 * ATTN_BLOCK), :], band_mask, HEAD_HALF)
                dst = pl.multiple_of(base + q0, ATTN_BLOCK)
                put(ocm_ref, pl.ds(dst, ATTN_BLOCK), o)
                put(lcm_ref, pl.ds(dst, ATTN_BLOCK), l)
                return c2

            if nb > 1:
                lax.fori_loop(1, nb, block_body, 0)
            return carry

        lax.fori_loop(0, d, class_body, 0)

        if g == 0:
            assert d == 1
            m_ref[...] = lcm_ref[...]
            n_ref[...] = ocm_ref[...]
            d_ref[...] = jnp.ones(d_ref.shape, F32)
        else:
            for r in range(d):
                rows = pl.ds(r, sd, stride=d)
                for t in range(n_lt):
                    o_r = ocm_ref[t, r * sd:(r + 1) * sd, :]
                    l_r = lcm_ref[t, r * sd:(r + 1) * sd, :]
                    m_old = m_ref[t, rows, :]
                    m_new = jnp.maximum(m_old, l_r)
                    a = jnp.exp(m_old - m_new)
                    b = jnp.exp(l_r - m_new)
                    n_ref[t, rows, :] = n_ref[t, rows, :] * a + o_r * b
                    d_ref[t, rows, :] = d_ref[t, rows, :] * a + b
                    m_ref[t, rows, :] = m_new
    for t in range(n_lt):
        out_ref[0, :, t * LANES:(t + 1) * LANES] = (n_ref[t] / d_ref[t]).astype(BF16)


def _attn_prompt(cm, batch, seq):
    in_specs = []
    for _kind in range(3):
        for (_, d) in ATTN_GROUPS:
            assert seq % (d * ATTN_BLOCK) == 0
            in_specs.append(pl.BlockSpec((1, d, seq // d, HALF_W), lambda b, hh: (b, 0, 0, hh)))
    return pl.pallas_call(
        functools.partial(_attn_prompt_kernel, seq),
        grid=(batch, GROUP_W // HALF_W),
        in_specs=in_specs,
        out_specs=pl.BlockSpec((1, seq, HALF_W), lambda b, hh: (b, 0, hh)),
        out_shape=jax.ShapeDtypeStruct((batch, seq, GROUP_W), BF16),
        scratch_shapes=[pltpu.VMEM((HALF_W // LANES, seq, LANES), F32) for _ in range(5)],
        compiler_params=_params("parallel", "parallel"),
        name="attn_prompt",
    )(*cm)


def _attn_sample_kernel(ns, qkv_ref, c0_ref, c1_ref, c2_ref, _n0, _n1, _n2, out_ref, new0_ref, new1_ref, new2_ref):
    cache_refs = (c0_ref, c1_ref, c2_ref)
    new_refs = (new0_ref, new1_ref, new2_ref)
    qkv = qkv_ref[...]
    row_pad = jnp.zeros((LANES - ns, GROUP_W), F32)
    lane = lax.broadcasted_iota(jnp.int32, (HEADS_PER_GROUP, GROUP_W), 1)
    hrow = lax.broadcasted_iota(jnp.int32, (HEADS_PER_GROUP, GROUP_W), 0)
    head_mask = (lane // HEAD_DIM == hrow).astype(F32)
    nr = ns * HEADS_PER_GROUP
    n_new = 16
    assert ns <= n_new
    zpad = jnp.zeros((n_new - ns, GROUP_W), F32)
    outs, lses = [], []
    for g, (window, d) in enumerate(ATTN_GROUPS):
        assert window % d == 0 and d & (d - 1) == 0 and cache_refs[g].shape[2:] == (2, GROUP_W, window)
        k_t = cache_refs[g][0, 0, 0].astype(BF16)
        v_t = cache_refs[g][0, 0, 1].astype(BF16)
        q = qkv[:, g * GROUP_W:(g + 1) * GROUP_W] * ATTN_SCALE
        k_new32 = qkv[:, QKV_W + g * GROUP_W:QKV_W + (g + 1) * GROUP_W]
        v_new32 = qkv[:, 2 * QKV_W + g * GROUP_W:2 * QKV_W + (g + 1) * GROUP_W]
        for kv, new32 in enumerate((k_new32, v_new32)):
            new_t = jnp.concatenate([new32, row_pad], axis=0).T
            new_refs[g][0, 0, kv] = jnp.concatenate([cache_refs[g][0, 0, kv][:, ns:], new_t[:, :ns]], axis=1)
        k_new = jnp.concatenate([k_new32, zpad], axis=0).astype(BF16)
        v_new = jnp.concatenate([v_new32, zpad], axis=0).astype(BF16)
        q_exp = jnp.concatenate([q[n:n + 1] * head_mask for n in range(ns)], axis=0).astype(BF16)
        s_c = jnp.dot(q_exp, k_t, preferred_element_type=F32)
        s_n = lax.dot_general(q_exp, k_new, NT_DIMS, preferred_element_type=F32)
        delta_c = (lax.broadcasted_iota(jnp.int32, (nr, window), 1)
                   - lax.broadcasted_iota(jnp.int32, (nr, window), 0) // HEADS_PER_GROUP)
        delta_n = (lax.broadcasted_iota(jnp.int32, (nr, n_new), 0) // HEADS_PER_GROUP
                   - lax.broadcasted_iota(jnp.int32, (nr, n_new), 1))
        s_c = jnp.where((delta_c >= 0) & ((delta_c & (d - 1)) == 0), s_c, NEG_INF)
        s_n = jnp.where((delta_n >= 0) & ((delta_n & (d - 1)) == 0), s_n, NEG_INF)
        m = jnp.maximum(jnp.max(s_c, axis=-1, keepdims=True), jnp.max(s_n, axis=-1, keepdims=True))
        p_c = jnp.exp(s_c - m)
        p_n = jnp.exp(s_n - m)
        den = jnp.sum(p_c, axis=-1, keepdims=True) + jnp.sum(p_n, axis=-1, keepdims=True)
        pv = (lax.dot_general(p_c.astype(BF16), v_t, NT_DIMS, preferred_element_type=F32)
              + jnp.dot(p_n.astype(BF16), v_new, preferred_element_type=F32)) / den
        lse = jnp.broadcast_to(m + jnp.log(den), pv.shape)
        o_rows, l_rows = [], []
        for n in range(ns):
            blk = slice(n * HEADS_PER_GROUP, (n + 1) * HEADS_PER_GROUP)
            o_rows.append(jnp.sum(pv[blk] * head_mask, axis=0, keepdims=True))
            l_rows.append(jnp.sum(lse[blk] * head_mask, axis=0, keepdims=True))
        outs.append(jnp.concatenate(o_rows, axis=0))
        lses.append(jnp.concatenate(l_rows, axis=0))
    m = jnp.maximum(jnp.maximum(lses[0], lses[1]), lses[2])
    es = [jnp.exp(l - m) for l in lses]
    num = es[0] * outs[0] + es[1] * outs[1] + es[2] * outs[2]
    out_ref[...] = num / (es[0] + es[1] + es[2])


def _attn_sample(qkv, caches, new_caches, layer, dec_batch, ns):
    assert ns % 8 == 0
    in_specs = [pl.BlockSpec((ns, 3 * QKV_W), lambda b: (b, 0))]
    out_specs = [pl.BlockSpec((ns, GROUP_W), lambda b: (b, 0))]
    out_shape = [jax.ShapeDtypeStruct((dec_batch * ns, GROUP_W), F32)]
    for c in caches:
        spec = pl.BlockSpec((1, 1) + c.shape[2:], lambda b, layer=layer: (layer, b, 0, 0, 0))
        in_specs.append(spec)
        out_specs.append(spec)
        out_shape.append(jax.ShapeDtypeStruct(c.shape, F32))
    in_specs += [pl.BlockSpec(memory_space=pl.ANY)] * len(new_caches)
    n_in = 1 + len(caches)
    return pl.pallas_call(
        functools.partial(_attn_sample_kernel, ns),
        grid=(dec_batch,),
        in_specs=in_specs,
        out_specs=out_specs,
        out_shape=out_shape,
        input_output_aliases={n_in + g: 1 + g for g in range(len(new_caches))},
        compiler_params=_params("arbitrary"),
        name="attn_sample",
    )(qkv, *caches, *new_caches)


def _conv_kernel(tc, has_hist, *refs):
    if has_hist:
        c_ref, hist_ref, dw_ref, bias_ref, lng_ref, lnb_ref, out_ref, ext_ref = refs
        ext_ref[0:CONV_HIST, :] = hist_ref[0]
        ext_ref[CONV_HIST:, :] = c_ref[...]
    else:
        c_ref, dw_ref, bias_ref, lng_ref, lnb_ref, out_ref, ext_ref = refs
        i = pl.program_id(1)
        t0 = pl.multiple_of(i * tc, 8)
        prev = pl.multiple_of(jnp.maximum(t0 - CONV_HIST, 0), 8)
        hist = c_ref[pl.ds(prev, CONV_HIST), :]
        ext_ref[0:CONV_HIST, :] = jnp.where(i > 0, hist, 0.0)
        ext_ref[CONV_HIST:, :] = c_ref[pl.ds(t0, tc), :]
    off = CONV_HIST - (CONV_WIDTH - 1)
    acc = ext_ref[pl.ds(off, tc), :] * dw_ref[0:1, :]
    for k in range(1, CONV_WIDTH):
        acc = acc + ext_ref[pl.ds(off + k, tc), :] * dw_ref[k:k + 1, :]
    y = acc + bias_ref[...]
    mu = jnp.mean(y, axis=-1, keepdims=True)
    yc = y - mu
    var = jnp.mean(yc * yc, axis=-1, keepdims=True)
    y = yc * lax.rsqrt(var + NORM_EPS) * lng_ref[...] + lnb_ref[...]
    out_ref[...] = y * _sigmoid(y)


def _conv_prompt(c, dw, bias, lng, lnb, batch, seq, tc):
    assert seq % tc == 0 and tc % 8 == 0
    nt = seq // tc
    vec = lambda n: pl.BlockSpec((1, n), lambda b, i: (0, 0))
    return pl.pallas_call(
        functools.partial(_conv_kernel, tc, False),
        grid=(batch, nt),
        in_specs=[pl.BlockSpec((seq, GROUP_W), lambda b, i: (b, 0)),
                  pl.BlockSpec((CONV_WIDTH, GROUP_W), lambda b, i: (0, 0)), vec(GROUP_W), vec(GROUP_W), vec(GROUP_W)],
        out_specs=pl.BlockSpec((tc, GROUP_W), lambda b, i, nt=nt: (b * nt + i, 0)),
        out_shape=jax.ShapeDtypeStruct((batch * seq, GROUP_W), F32),
        scratch_shapes=[pltpu.VMEM((tc + CONV_HIST, GROUP_W), F32)],
        compiler_params=_params("parallel", "arbitrary"),
        name="conv_prompt",
    )(c, dw, bias, lng, lnb)


def _conv_sample(c, hist, dw, bias, lng, lnb, dec_batch, ns):
    vec = lambda n: pl.BlockSpec((1, n), lambda b: (0, 0))
    return pl.pallas_call(
        functools.partial(_conv_kernel, ns, True),
        grid=(dec_batch,),
        in_specs=[pl.BlockSpec((ns, GROUP_W), lambda b: (b, 0)),
                  pl.BlockSpec((1, CONV_HIST, GROUP_W), lambda b: (b, 0, 0)),
                  pl.BlockSpec((CONV_WIDTH, GROUP_W), lambda b: (0, 0)), vec(GROUP_W), vec(GROUP_W), vec(GROUP_W)],
        out_specs=pl.BlockSpec((ns, GROUP_W), lambda b: (b, 0)),
        out_shape=jax.ShapeDtypeStruct((dec_batch * ns, GROUP_W), F32),
        scratch_shapes=[pltpu.VMEM((ns + CONV_HIST, GROUP_W), F32)],
        compiler_params=_params("parallel"),
        name="conv_sample",
    )(c, hist, dw, bias, lng, lnb)


def _merge_kernel(x_ref, attn_ref, u_ref, ga_ref, gb_ref, wa_ref, wc_ref, bc_ref, wo_ref, out_ref):
    y_a = jnp.dot(attn_ref[...].astype(BF16), wa_ref[...], preferred_element_type=F32)
    y_b = jnp.dot(u_ref[...].astype(BF16), wc_ref[...], preferred_element_type=F32) + bc_ref[...]
    y = (ga_ref[...] * y_a + gb_ref[...] * y_b).astype(BF16)
    out_ref[...] = x_ref[...] + jnp.dot(y, wo_ref[...], preferred_element_type=F32)


def _merge(x, attn, u, ga, gb, wa, wc, bc, wo, *, tm):
    t, d_model = x.shape
    assert t % tm == 0
    row = lambda i: (i, 0)
    const = lambda i: (0, 0)
    return pl.pallas_call(
        _merge_kernel,
        grid=(t // tm,),
        in_specs=[pl.BlockSpec((tm, d_model), row), pl.BlockSpec((tm, GROUP_W), row), pl.BlockSpec((tm, GROUP_W), row),
                  pl.BlockSpec((tm, d_model), row), pl.BlockSpec((tm, d_model), row),
                  pl.BlockSpec((GROUP_W, d_model), const), pl.BlockSpec((GROUP_W, d_model), const),
                  pl.BlockSpec((1, d_model), const), pl.BlockSpec((d_model, d_model), const)],
        out_specs=pl.BlockSpec((tm, d_model), row),
        out_shape=jax.ShapeDtypeStruct((t, d_model), F32),
        compiler_params=_params("parallel"),
        name="merge",
    )(x, attn, u, ga, gb, wa, wc, bc, wo)


_NO_RANK = 64.0


def _top_values(s, out_ref, n, want_rank=False):
    cur = s
    rank = jnp.full(s.shape, _NO_RANK, F32) if want_rank else None
    for j in range(n):
        m = jnp.max(cur, axis=0, keepdims=True)
        out_ref[j:j + 1, :] = m
        hit = cur == m
        if want_rank:
            rank = jnp.where(hit, float(j), rank)
        if j + 1 < n:
            cur = jnp.where(hit, -jnp.inf, cur)
    return rank


def _split_bf16(x):
    hi = x.astype(BF16)
    return hi, (x - hi.astype(F32)).astype(BF16)


def _dot_bf16x3(a_hi, a_lo, b_hi, b_lo):
    d = functools.partial(jnp.dot, preferred_element_type=F32)
    return d(a_hi, b_hi) + (d(a_hi, b_lo) + d(a_lo, b_hi))


_CAND = [(i, PEER_TOPK // (i + 1)) for i in range(PEER_TOPK)]
_CAND_ROWS = 64
PEER_SLAB = 256


def _peer_kernel(tp, ech, lb, x_ref, g_ref, wqh_ref, wql_ref, kh_ref, kl_ref, u_ref, vt_ref, out_ref,
                 ht_ref, qh_ref, ql_ref, cnt_ref, e1_ref, s2_ref, rank_ref, e2_ref, at_ref, gt_ref, acc_ref,
                 a_ref, b_ref, cand_ref, top_ref):
    c = pl.program_id(1)
    n_lb = tp // lb

    @pl.when(c == 0)
    def _route():
        h2 = _rmsnorm(x_ref[...], g_ref[...])
        ht = h2.T
        ht_hi, ht_lo = _split_bf16(ht)
        ht_ref[...] = ht_hi
        qt = _dot_bf16x3(wqh_ref[...], wql_ref[...], ht_hi, ht_lo)
        q_hi, q_lo = _split_bf16(qt)
        qh_ref[...] = q_hi.reshape(qh_ref.shape)
        ql_ref[...] = q_lo.reshape(ql_ref.shape)
        acc_ref[...] = jnp.zeros(acc_ref.shape, F32)
        gt_ref[1] = jnp.zeros(gt_ref.shape[1:], BF16)

        def head_body(h, carry):
            k1, k2 = 2 * h, 2 * h + 1
            cnt_ref[h] = _dot_bf16x3(kh_ref[k1], kl_ref[k1], qh_ref[k1], ql_ref[k1])
            s2_ref[...] = _dot_bf16x3(kh_ref[k2], kl_ref[k2], qh_ref[k2], ql_ref[k2])
            for j in range(n_lb):
                cols = slice(j * lb, (j + 1) * lb)
                s1 = cnt_ref[h, :, cols]
                s2 = s2_ref[:, cols]
                _top_values(s1, a_ref, PEER_TOPK)
                rank2 = _top_values(s2, b_ref, PEER_TOPK, want_rank=True)
                cand_ref[...] = jnp.full(cand_ref.shape, -jnp.inf, F32)
                row = 0
                for (i, n) in _CAND:
                    cand_ref[row:row + n, :] = a_ref[i:i + 1, :] + b_ref[0:n, :]
                    row += n
                _top_values(cand_ref[...], top_ref, PEER_TOPK)
                top = top_ref[...]
                tau = top[PEER_TOPK - 1:PEER_TOPK, :]
                z = jnp.sum(jnp.exp(top - top[0:1, :]), axis=0, keepdims=True)
                a0, a15 = a_ref[0:1, :], a_ref[PEER_TOPK - 1:PEER_TOPK, :]
                b0 = b_ref[0:1, :]
                cnt = jnp.zeros(s1.shape, F32)
                for r in range(PEER_TOPK):
                    cnt = cnt + jnp.where(s1 + b_ref[r:r + 1, :] >= tau, 1.0, 0.0)
                cnt_ref[h, :, cols] = jnp.where(s1 >= a15, cnt, 0.0)
                e1_ref[h, :, cols] = jnp.exp(s1 - a0)
                rank_ref[h, :, cols] = rank2.astype(BF16)
                e2_ref[h, :, cols] = (jnp.exp(s2 - b0) / z).astype(BF16)
            return carry

        lax.fori_loop(0, PEER_HEADS, head_body, 0)

    n_chunks = pl.num_programs(1) - 1
    cur = c % 2
    prev = 1 - cur
    n_slabs = ech // PEER_SLAB
    d_part = acc_ref.shape[0] // n_slabs

    def output_part(p):
        rows = slice(p * d_part, (p + 1) * d_part)
        acc_ref[rows, :] += jnp.dot(vt_ref[rows, :], gt_ref[prev], preferred_element_type=F32)

    @pl.when(c < n_chunks)
    def _chunk():
        i1_rows = pl.ds(pl.multiple_of(c * (ech // N_KEYS), 8), ech // N_KEYS)

        def pre_activations(s0):
            slab = slice(s0, s0 + PEER_SLAB)
            at_ref[slab, :] = jnp.dot(u_ref[slab, :], ht_ref[...], preferred_element_type=F32)

        pre_activations(0)
        for si in range(n_slabs):
            s0 = si * PEER_SLAB
            if si + 1 < n_slabs:
                pre_activations(s0 + PEER_SLAB)
            output_part(si)
            for i1l in range(s0 // N_KEYS, (s0 + PEER_SLAB) // N_KEYS):
                rows = slice(i1l * N_KEYS, (i1l + 1) * N_KEYS)
                for j in range(n_lb):
                    cols = slice(j * lb, (j + 1) * lb)
                    w = jnp.zeros((N_KEYS, lb), BF16)
                    for h in range(PEER_HEADS):
                        cnt_b = jnp.broadcast_to(cnt_ref[h, i1_rows, cols][i1l:i1l + 1], (N_KEYS, lb)).astype(BF16)
                        e1_b = jnp.broadcast_to(e1_ref[h, i1_rows, cols][i1l:i1l + 1], (N_KEYS, lb)).astype(BF16)
                        keep = rank_ref[h, :, cols] < cnt_b
                        w = w + jnp.where(keep, e2_ref[h, :, cols] * e1_b, jnp.zeros((), BF16))
                    a = at_ref[rows, cols]
                    act = 0.5 * a * (1.0 + lax.erf(a * (0.5 ** 0.5)))
                    gt_ref[cur, rows, cols] = w * act.astype(BF16)

    @pl.when(c == n_chunks)
    def _drain():
        for p in range(n_slabs):
            output_part(p)
        out_ref[...] = x_ref[...] + acc_ref[...].T


def _peer(x, g, wq_t, keys, u_bf16, vt_bf16, *, tp, ech=8 * N_KEYS, lb=LANES):
    t, d_model = x.shape
    n_exp = u_bf16.shape[0]
    assert t % tp == 0 and n_exp % ech == 0 and ech == 8 * N_KEYS and tp % lb == 0 and n_exp == N_KEYS * N_KEYS
    q_w = wq_t[0].shape[0]
    assert q_w == PEER_HEADS * 2 * N_KEYS and ech % PEER_SLAB == 0 and d_model % (ech // PEER_SLAB) == 0
    n_chunks = n_exp // ech
    meta = lambda dt: pltpu.VMEM((PEER_HEADS, N_KEYS, tp), dt)
    wq_spec = pl.BlockSpec((q_w, d_model), lambda i, c: (0, 0), pipeline_mode=pl.Buffered(1))
    keys_spec = pl.BlockSpec((2 * PEER_HEADS, N_KEYS, N_KEYS), lambda i, c: (0, 0, 0), pipeline_mode=pl.Buffered(1))
    return pl.pallas_call(
        functools.partial(_peer_kernel, tp, ech, lb),
        grid=(t // tp, n_chunks + 1),
        in_specs=[pl.BlockSpec((tp, d_model), lambda i, c: (i, 0)),
                  pl.BlockSpec((1, d_model), lambda i, c: (0, 0)),
                  wq_spec, wq_spec, keys_spec, keys_spec,
                  pl.BlockSpec((ech, d_model), lambda i, c: (jnp.minimum(c, n_chunks - 1), 0)),
                  pl.BlockSpec((d_model, ech), lambda i, c: (0, jnp.maximum(c - 1, 0)))],
        out_specs=pl.BlockSpec((tp, d_model), lambda i, c: (i, 0)),
        out_shape=jax.ShapeDtypeStruct((t, d_model), F32),
        scratch_shapes=[pltpu.VMEM((d_model, tp), BF16),
                        pltpu.VMEM((2 * PEER_HEADS, N_KEYS, tp), BF16),
                        pltpu.VMEM((2 * PEER_HEADS, N_KEYS, tp), BF16),
                        meta(F32), meta(F32),
                        pltpu.VMEM((N_KEYS, tp), F32),
                        meta(BF16), meta(BF16),
                        pltpu.VMEM((ech, tp), F32), pltpu.VMEM((2, ech, tp), BF16),
                        pltpu.VMEM((d_model, tp), F32),
                        pltpu.VMEM((PEER_TOPK, lb), F32), pltpu.VMEM((PEER_TOPK, lb), F32),
                        pltpu.VMEM((_CAND_ROWS, lb), F32), pltpu.VMEM((PEER_TOPK, lb), F32)],
        compiler_params=_params("parallel", "arbitrary"),
        name="peer",
    )(x, g, wq_t[0], wq_t[1], keys[0], keys[1], u_bf16, vt_bf16)


def _final_norm_kernel(x_ref, g_ref, out_ref):
    out_ref[...] = _rmsnorm(x_ref[...], g_ref[...])


def _final_norm(x, g, *, tm):
    t, d_model = x.shape
    return pl.pallas_call(
        _final_norm_kernel,
        grid=(t // tm,),
        in_specs=[pl.BlockSpec((tm, d_model), lambda i: (i, 0)), pl.BlockSpec((1, d_model), lambda i: (0, 0))],
        out_specs=pl.BlockSpec((tm, d_model), lambda i: (i, 0)),
        out_shape=jax.ShapeDtypeStruct((t, d_model), F32),
        compiler_params=_params("parallel"),
        name="final_norm",
    )(x, g)


def _time_last(buf):
    return buf.transpose(0, 1, 3, 4, 5, 2).reshape(buf.shape[0], buf.shape[1], 2, GROUP_W, buf.shape[2])


def _time_first(buf):
    depth, batch, _, _, time = buf.shape
    return buf.reshape(depth, batch, 2, HEADS_PER_GROUP, HEAD_DIM, time).transpose(0, 1, 5, 2, 3, 4)


def kernel(x_prompt, x_sample, cache_kv_w128, cache_kv_w512, cache_kv_w2048, state_conv, norm_mix_g, norm_ffn_g, w_in, dw_kernel, dw_bias, conv_ln_g, conv_ln_b, w_conv_out, b_conv_out, w_attn_out, w_out, w_peer_q, peer_sub_keys, peer_u, peer_v, norm_final_g):
    batch, seq, d_model = x_prompt.shape
    dec_batch, ns, _ = x_sample.shape
    depth = w_in.shape[0]
    tp_tokens, ts_tokens = batch * seq, dec_batch * ns
    tm_p = 512
    cache_in = (cache_kv_w128, cache_kv_w512, cache_kv_w2048)
    for c, (window, _) in zip(cache_in, ATTN_GROUPS):
        assert c.shape[2] == window
    caches = [_time_last(c) for c in cache_in]
    assert state_conv.shape[2] == CONV_WIDTH - 1
    hist_pad = CONV_HIST - (CONV_WIDTH - 1)

    xp = x_prompt.reshape(tp_tokens, d_model)
    xs = x_sample.reshape(ts_tokens, d_model)
    tails = [jnp.zeros((depth, batch, 2, GROUP_W, min(window, seq)), F32) for (window, _) in ATTN_GROUPS]
    new_caches = [jnp.zeros(c.shape, F32) for c in caches]
    conv_p, conv_s = [], []
    vec = lambda a: a.reshape(1, -1)
    for l in range(depth):
        w_in_b = w_in[l].astype(BF16)
        wa_b, wc_b, wo_b = w_attn_out[l].astype(BF16), w_conv_out[l].astype(BF16), w_out[l].astype(BF16)
        wq_t = _split_bf16(w_peer_q[l].T)
        keys = _split_bf16(peer_sub_keys[l].reshape(2 * PEER_HEADS, N_KEYS, -1))
        u_b = peer_u[l].astype(BF16)
        vt_b = peer_v[l].T.astype(BF16)
        conv_w = (dw_kernel[l], vec(dw_bias[l]), vec(conv_ln_g[l]), vec(conv_ln_b[l]))
        mix_g, ffn_g = vec(norm_mix_g[l]), vec(norm_ffn_g[l])

        outs = _in_proj(xp, mix_g, w_in_b, tm=tm_p, class_major=True, batch=batch, seq=seq, layer=l, tails=tails)
        c, ga, gb = outs[:3]
        tails = list(outs[12:])
        attn = _attn_prompt(outs[3:12], batch, seq).reshape(tp_tokens, GROUP_W)
        u = _conv_prompt(c, *conv_w, batch, seq, 512)
        xp = _merge(xp, attn, u, ga, gb, wa_b, wc_b, vec(b_conv_out[l]), wo_b, tm=tm_p)
        xp = _peer(xp, ffn_g, wq_t, keys, u_b, vt_b, tp=512)
        conv_p.append(c.reshape(batch, seq, GROUP_W)[:, seq - (CONV_WIDTH - 1):])

        qkv, c, ga, gb = _in_proj(xs, mix_g, w_in_b, tm=ts_tokens, class_major=False)
        attn, *new_caches = _attn_sample(qkv, caches, new_caches, l, dec_batch, ns)
        hist = jnp.pad(state_conv[l], ((0, 0), (hist_pad, 0), (0, 0)))
        u = _conv_sample(c, hist, *conv_w, dec_batch, ns)
        xs = _merge(xs, attn, u, ga, gb, wa_b, wc_b, vec(b_conv_out[l]), wo_b, tm=ts_tokens)
        xs = _peer(xs, ffn_g, wq_t, keys, u_b, vt_b, tp=ts_tokens)
        full = jnp.concatenate([state_conv[l], c.reshape(dec_batch, ns, GROUP_W)], axis=1)
        conv_s.append(full[:, full.shape[1] - (CONV_WIDTH - 1):])

    y_prompt = _final_norm(xp, vec(norm_final_g), tm=tm_p).reshape(batch, seq, d_model)
    y_sample = _final_norm(xs, vec(norm_final_g), tm=ts_tokens).reshape(dec_batch, ns, d_model)
    return (y_prompt, y_sample,
            _time_first(tails[0]), _time_first(tails[1]), _time_first(tails[2]), jnp.stack(conv_p),
            _time_first(new_caches[0]), _time_first(new_caches[1]), _time_first(new_caches[2]), jnp.stack(conv_s))
```

```python
import functools

import jax
import jax.numpy as jnp
from jax import lax
from jax.experimental import pallas as pl
from jax.experimental.pallas import tpu as pltpu

F32 = jnp.float32
BF16 = jnp.bfloat16

HEAD_DIM = 64
HEADS_PER_GROUP = 8
ATTN_GROUPS = ((128, 1), (512, 4), (2048, 16))
N_GROUPS = len(ATTN_GROUPS)
GROUP_W = HEADS_PER_GROUP * HEAD_DIM
QKV_W = N_GROUPS * GROUP_W
ATTN_BLOCK = 128
ATTN_SCALE = HEAD_DIM ** -0.5
CONV_WIDTH = 31
CONV_HIST = 32
N_KEYS = 128
PEER_HEADS = 8
PEER_TOPK = 16
NORM_EPS = 1e-6
NEG_INF = -1e30

VMEM_LIMIT_BYTES = 56 * 1024 * 1024
LANES = 128
HEAD_HALF = 4
ATTN_UNROLL = 4
HALF_W = HEAD_HALF * HEAD_DIM

NT_DIMS = (((1,), (1,)), ((), ()))


def _params(*sem, flags=None):
    return pltpu.CompilerParams(dimension_semantics=sem, vmem_limit_bytes=VMEM_LIMIT_BYTES, flags=flags)


def _rmsnorm(x, g):
    return x * lax.rsqrt(jnp.mean(x * x, axis=-1, keepdims=True) + NORM_EPS) * g


def _sigmoid(x):
    return 1.0 / (1.0 + jnp.exp(-x))


def _in_proj_kernel(class_major, tm, seq, x_ref, g_ref, w_ref, *refs):
    if class_major:
        c_ref, ga_ref, gb_ref = refs[3:6]
        cm_refs = refs[6:15]
        tail_refs = refs[15:18]
        scr_ref = refs[18]
        tps = seq // tm
        last_tile = pl.program_id(0) % tps == tps - 1
    else:
        qkv_ref, c_ref, ga_ref, gb_ref = refs[:4]
    h = _rmsnorm(x_ref[...], g_ref[...]).astype(BF16)
    glu_a = None
    for j in range(w_ref.shape[1] // GROUP_W):
        res = jnp.dot(h, w_ref[:, j * GROUP_W:(j + 1) * GROUP_W], preferred_element_type=F32)
        if j < 9:
            kind, g = divmod(j, N_GROUPS)
            if class_major:
                if kind > 0:
                    keep = tail_refs[g].shape[-1]
                    if min(ATTN_GROUPS[g][0], seq) == seq:
                        tail_refs[g][0, 0, kind - 1] = res.T
                    else:
                        @pl.when(last_tile)
                        def _(res=res, g=g, kind=kind, keep=keep):
                            tail_refs[g][0, 0, kind - 1] = res[tm - keep:, :].T
                val = res * ATTN_SCALE if kind == 0 else res
                d = ATTN_GROUPS[g][1]
                cm = cm_refs[kind * N_GROUPS + g]
                if d == 1:
                    cm[0, 0] = val.astype(BF16)
                else:
                    for t in range(GROUP_W // LANES):
                        scr_ref[t] = val[:, t * LANES:(t + 1) * LANES]
                    for r in range(d):
                        cm[0, r] = jnp.concatenate(
                            [scr_ref[t, pl.ds(r, tm // d, stride=d), :] for t in range(GROUP_W // LANES)], axis=1).astype(BF16)
            else:
                qkv_ref[:, j * GROUP_W:(j + 1) * GROUP_W] = res
        elif j == 9:
            glu_a = res
        elif j == 10:
            c_ref[...] = glu_a * _sigmoid(res)
        elif j < 13:
            ga_ref[:, (j - 11) * GROUP_W:(j - 10) * GROUP_W] = _sigmoid(res)
        else:
            gb_ref[:, (j - 13) * GROUP_W:(j - 12) * GROUP_W] = _sigmoid(res)


def _in_proj(x, g, w_bf16, *, tm, class_major, batch=None, seq=None, layer=None, tails=None):
    t, d_model = x.shape
    in_w = w_bf16.shape[1]
    assert t % tm == 0 and in_w == 3 * QKV_W + 2 * GROUP_W + 2 * d_model and d_model == 2 * GROUP_W
    nt = t // tm
    row = lambda i: (i, 0)
    in_specs = [
        pl.BlockSpec((tm, d_model), row),
        pl.BlockSpec((1, d_model), lambda i: (0, 0)),
        pl.BlockSpec((d_model, in_w), lambda i: (0, 0), pipeline_mode=pl.Buffered(1)),
    ]
    common_shapes = [
        jax.ShapeDtypeStruct((t, GROUP_W), F32),
        jax.ShapeDtypeStruct((t, d_model), F32),
        jax.ShapeDtypeStruct((t, d_model), F32),
    ]
    common_specs = [pl.BlockSpec((tm, GROUP_W), row), pl.BlockSpec((tm, d_model), row), pl.BlockSpec((tm, d_model), row)]
    aliases = {}
    args = (x, g, w_bf16)
    if class_major:
        assert seq % tm == 0
        tps = seq // tm
        in_specs += [pl.BlockSpec(memory_space=pl.ANY)] * N_GROUPS
        args += tuple(tails)
        out_shape = list(common_shapes)
        out_specs = list(common_specs)
        for _kind in range(3):
            for (_, d) in ATTN_GROUPS:
                assert tm % (d * 16) == 0
                out_shape.append(jax.ShapeDtypeStruct((batch, d, seq // d, GROUP_W), BF16))
                out_specs.append(pl.BlockSpec((1, d, tm // d, GROUP_W), lambda i, tps=tps: (i // tps, 0, i % tps, 0)))
        for g_idx, ((window, _), tail) in enumerate(zip(ATTN_GROUPS, tails)):
            keep = min(window, seq)
            assert tail.shape[1:] == (batch, 2, GROUP_W, keep) and (keep == seq or (keep <= tm and keep % LANES == 0))
            aliases[3 + g_idx] = len(out_shape)
            out_shape.append(jax.ShapeDtypeStruct(tail.shape, F32))
            if keep == seq:
                out_specs.append(pl.BlockSpec((1, 1, 2, GROUP_W, tm), lambda i, tps=tps: (layer, i // tps, 0, 0, i % tps)))
            else:
                out_specs.append(pl.BlockSpec((1, 1, 2, GROUP_W, keep), lambda i, tps=tps: (layer, i // tps, 0, 0, 0)))
        scratch = [pltpu.VMEM((GROUP_W // LANES, tm, LANES), F32)]
    else:
        out_shape = [jax.ShapeDtypeStruct((t, 3 * QKV_W), F32)] + common_shapes
        out_specs = [pl.BlockSpec((tm, 3 * QKV_W), row)] + common_specs
        scratch = []
    return pl.pallas_call(
        functools.partial(_in_proj_kernel, class_major, tm, seq),
        grid=(nt,),
        in_specs=in_specs,
        out_specs=out_specs,
        out_shape=out_shape,
        scratch_shapes=scratch,
        input_output_aliases=aliases,
        compiler_params=_params("arbitrary"),
        name="in_proj_cm" if class_major else "in_proj_nat",
    )(*args)


def _attend_heads(qb, kb, vb, mask, n_heads):
    outs, lses = [], []
    for h in range(n_heads):
        sl = slice(h * HEAD_DIM, (h + 1) * HEAD_DIM)
        s = lax.dot_general(qb[:, sl], kb[:, sl], NT_DIMS, preferred_element_type=F32)
        s = jnp.where(mask, s, NEG_INF)
        m = jnp.max(s, axis=-1, keepdims=True)
        p = jnp.exp(s - m)
        den = jnp.sum(p, axis=-1, keepdims=True)
        o = jnp.dot(p.astype(BF16), vb[:, sl], preferred_element_type=F32) / den
        outs.append(o)
        lses.append(jnp.broadcast_to(m + jnp.log(den), o.shape))
    return jnp.concatenate(outs, axis=1), jnp.concatenate(lses, axis=1)


def _attn_prompt_kernel(seq, *refs):
    qkv_refs = refs[:9]
    out_ref = refs[9]
    ocm_ref, lcm_ref, m_ref, n_ref, d_ref = refs[10:]
    n_lt = HALF_W // LANES
    dist = (lax.broadcasted_iota(jnp.int32, (ATTN_BLOCK, 2 * ATTN_BLOCK), 0) + ATTN_BLOCK
            - lax.broadcasted_iota(jnp.int32, (ATTN_BLOCK, 2 * ATTN_BLOCK), 1))
    band_mask = (dist >= 0) & (dist <= ATTN_BLOCK)
    causal_mask = (lax.broadcasted_iota(jnp.int32, (ATTN_BLOCK, ATTN_BLOCK), 0)
                   >= lax.broadcasted_iota(jnp.int32, (ATTN_BLOCK, ATTN_BLOCK), 1))

    def put(ref, rows, val):
        for t in range(n_lt):
            ref[t, rows, :] = val[:, t * LANES:(t + 1) * LANES]

    for g, (window, d) in enumerate(ATTN_GROUPS):
        assert window // d == ATTN_BLOCK
        q_ref, k_ref, v_ref = qkv_refs[g], qkv_refs[N_GROUPS + g], qkv_refs[2 * N_GROUPS + g]
        sd = seq // d
        nb = sd // ATTN_BLOCK

        def block(r, i, q_ref=q_ref, k_ref=k_ref, v_ref=v_ref, sd=sd):
            if isinstance(i, int) and i == 0:
                q0 = 0
                o, l = _attend_heads(q_ref[0, r, 0:ATTN_BLOCK, :], k_ref[0, r, 0:ATTN_BLOCK, :],
                                     v_ref[0, r, 0:ATTN_BLOCK, :], causal_mask, HEAD_HALF)
            else:
                q0, k0 = i * ATTN_BLOCK, (i - 1) * ATTN_BLOCK
                if not isinstance(i, int):
                    q0, k0 = pl.multiple_of(q0, ATTN_BLOCK), pl.multiple_of(k0, ATTN_BLOCK)
                o, l = _attend_heads(q_ref[0, r, pl.ds(q0, ATTN_BLOCK), :], k_ref[0, r, pl.ds(k0, 2 * ATTN_BLOCK), :],
                                     v_ref[0, r, pl.ds(k0, 2 * ATTN_BLOCK), :], band_mask, HEAD_HALF)
            dst = r * sd + q0
            if not isinstance(dst, int):
                dst = pl.multiple_of(dst, ATTN_BLOCK)
            put(ocm_ref, pl.ds(dst, ATTN_BLOCK), o)
            put(lcm_ref, pl.ds(dst, ATTN_BLOCK), l)

        if nb == 1:
            assert d % ATTN_UNROLL == 0

            def classes(it, carry, block=block):
                for u in range(ATTN_UNROLL):
                    block(it * ATTN_UNROLL + u, 0)
                return carry

            lax.fori_loop(0, d // ATTN_UNROLL, classes, 0)
        else:
            assert nb % ATTN_UNROLL == 0

            def one_class(r, carry, block=block, nb=nb):
                for i in range(ATTN_UNROLL):
                    block(r, i)

                def later_blocks(it, c2):
                    for u in range(ATTN_UNROLL):
                        block(r, (it + 1) * ATTN_UNROLL + u)
                    return c2

                if nb > ATTN_UNROLL:
                    lax.fori_loop(0, nb // ATTN_UNROLL - 1, later_blocks, 0)
                return carry

            if d == 1:
                one_class(0, 0)
            else:
                lax.fori_loop(0, d, one_class, 0)

        if g == 0:
            assert d == 1
            m_ref[...] = lcm_ref[...]
            n_ref[...] = ocm_ref[...]
            d_ref[...] = jnp.ones(d_ref.shape, F32)
        else:
            for r in range(d):
                rows = pl.ds(r, sd, stride=d)
                for t in range(n_lt):
                    o_r = ocm_ref[t, r * sd:(r + 1) * sd, :]
                    l_r = lcm_ref[t, r * sd:(r + 1) * sd, :]
                    m_old = m_ref[t, rows, :]
                    m_new = jnp.maximum(m_old, l_r)
                    a = jnp.exp(m_old - m_new)
                    b = jnp.exp(l_r - m_new)
                    n_ref[t, rows, :] = n_ref[t, rows, :] * a + o_r * b
                    d_ref[t, rows, :] = d_ref[t, rows, :] * a + b
                    m_ref[t, rows, :] = m_new
    for t in range(n_lt):
        out_ref[0, :, t * LANES:(t + 1) * LANES] = (n_ref[t] / d_ref[t]).astype(BF16)


def _attn_prompt(cm, batch, seq):
    in_specs = []
    for _kind in range(3):
        for (_, d) in ATTN_GROUPS:
            assert seq % (d * ATTN_BLOCK) == 0
            in_specs.append(pl.BlockSpec((1, d, seq // d, HALF_W), lambda b, hh: (b, 0, 0, hh)))
    return pl.pallas_call(
        functools.partial(_attn_prompt_kernel, seq),
        grid=(batch, GROUP_W // HALF_W),
        in_specs=in_specs,
        out_specs=pl.BlockSpec((1, seq, HALF_W), lambda b, hh: (b, 0, hh)),
        out_shape=jax.ShapeDtypeStruct((batch, seq, GROUP_W), BF16),
        scratch_shapes=[pltpu.VMEM((HALF_W // LANES, seq, LANES), F32) for _ in range(5)],
        compiler_params=_params("parallel", "parallel"),
        name="attn_prompt",
    )(*cm)


def _attn_sample_kernel(ns, qkv_ref, c0_ref, c1_ref, c2_ref, _n0, _n1, _n2, out_ref, new0_ref, new1_ref, new2_ref):
    cache_refs = (c0_ref, c1_ref, c2_ref)
    new_refs = (new0_ref, new1_ref, new2_ref)
    qkv = qkv_ref[...]
    row_pad = jnp.zeros((LANES - ns, GROUP_W), F32)
    lane = lax.broadcasted_iota(jnp.int32, (HEADS_PER_GROUP, GROUP_W), 1)
    hrow = lax.broadcasted_iota(jnp.int32, (HEADS_PER_GROUP, GROUP_W), 0)
    head_mask = (lane // HEAD_DIM == hrow).astype(F32)
    nr = ns * HEADS_PER_GROUP
    n_new = 16
    assert ns <= n_new
    zpad = jnp.zeros((n_new - ns, GROUP_W), F32)
    outs, lses = [], []
    for g, (window, d) in enumerate(ATTN_GROUPS):
        assert window % d == 0 and d & (d - 1) == 0 and cache_refs[g].shape[2:] == (2, GROUP_W, window)
        k_t = cache_refs[g][0, 0, 0].astype(BF16)
        v_t = cache_refs[g][0, 0, 1].astype(BF16)
        q = qkv[:, g * GROUP_W:(g + 1) * GROUP_W] * ATTN_SCALE
        k_new32 = qkv[:, QKV_W + g * GROUP_W:QKV_W + (g + 1) * GROUP_W]
        v_new32 = qkv[:, 2 * QKV_W + g * GROUP_W:2 * QKV_W + (g + 1) * GROUP_W]
        for kv, new32 in enumerate((k_new32, v_new32)):
            new_t = jnp.concatenate([new32, row_pad], axis=0).T
            new_refs[g][0, 0, kv] = jnp.concatenate([cache_refs[g][0, 0, kv][:, ns:], new_t[:, :ns]], axis=1)
        k_new = jnp.concatenate([k_new32, zpad], axis=0).astype(BF16)
        v_new = jnp.concatenate([v_new32, zpad], axis=0).astype(BF16)
        q_exp = jnp.concatenate([q[n:n + 1] * head_mask for n in range(ns)], axis=0).astype(BF16)
        s_c = jnp.dot(q_exp, k_t, preferred_element_type=F32)
        s_n = lax.dot_general(q_exp, k_new, NT_DIMS, preferred_element_type=F32)
        delta_c = (lax.broadcasted_iota(jnp.int32, (nr, window), 1)
                   - lax.broadcasted_iota(jnp.int32, (nr, window), 0) // HEADS_PER_GROUP)
        delta_n = (lax.broadcasted_iota(jnp.int32, (nr, n_new), 0) // HEADS_PER_GROUP
                   - lax.broadcasted_iota(jnp.int32, (nr, n_new), 1))
        s_c = jnp.where((delta_c >= 0) & ((delta_c & (d - 1)) == 0), s_c, NEG_INF)
        s_n = jnp.where((delta_n >= 0) & ((delta_n & (d - 1)) == 0), s_n, NEG_INF)
        m = jnp.maximum(jnp.max(s_c, axis=-1, keepdims=True), jnp.max(s_n, axis=-1, keepdims=True))
        p_c = jnp.exp(s_c - m)
        p_n = jnp.exp(s_n - m)
        den = jnp.sum(p_c, axis=-1, keepdims=True) + jnp.sum(p_n, axis=-1, keepdims=True)
        pv = (lax.dot_general(p_c.astype(BF16), v_t, NT_DIMS, preferred_element_type=F32)
              + jnp.dot(p_n.astype(BF16), v_new, preferred_element_type=F32)) / den
        lse = jnp.broadcast_to(m + jnp.log(den), pv.shape)
        o_rows, l_rows = [], []
        for n in range(ns):
            blk = slice(n * HEADS_PER_GROUP, (n + 1) * HEADS_PER_GROUP)
            o_rows.append(jnp.sum(pv[blk] * head_mask, axis=0, keepdims=True))
            l_rows.append(jnp.sum(lse[blk] * head_mask, axis=0, keepdims=True))
        outs.append(jnp.concatenate(o_rows, axis=0))
        lses.append(jnp.concatenate(l_rows, axis=0))
    m = jnp.maximum(jnp.maximum(lses[0], lses[1]), lses[2])
    es = [jnp.exp(l - m) for l in lses]
    num = es[0] * outs[0] + es[1] * outs[1] + es[2] * outs[2]
    out_ref[...] = num / (es[0] + es[1] + es[2])


def _attn_sample(qkv, caches, new_caches, layer, dec_batch, ns):
    assert ns % 8 == 0
    in_specs = [pl.BlockSpec((ns, 3 * QKV_W), lambda b: (b, 0))]
    out_specs = [pl.BlockSpec((ns, GROUP_W), lambda b: (b, 0))]
    out_shape = [jax.ShapeDtypeStruct((dec_batch * ns, GROUP_W), F32)]
    for c in caches:
        spec = pl.BlockSpec((1, 1) + c.shape[2:], lambda b, layer=layer: (layer, b, 0, 0, 0))
        in_specs.append(spec)
        out_specs.append(spec)
        out_shape.append(jax.ShapeDtypeStruct(c.shape, F32))
    in_specs += [pl.BlockSpec(memory_space=pl.ANY)] * len(new_caches)
    n_in = 1 + len(caches)
    return pl.pallas_call(
        functools.partial(_attn_sample_kernel, ns),
        grid=(dec_batch,),
        in_specs=in_specs,
        out_specs=out_specs,
        out_shape=out_shape,
        input_output_aliases={n_in + g: 1 + g for g in range(len(new_caches))},
        compiler_params=_params("arbitrary"),
        name="attn_sample",
    )(qkv, *caches, *new_caches)


def _conv_kernel(tc, has_hist, *refs):
    if has_hist:
        c_ref, hist_ref, dw_ref, bias_ref, lng_ref, lnb_ref, out_ref, ext_ref = refs
        ext_ref[0:CONV_HIST, :] = hist_ref[0]
        ext_ref[CONV_HIST:, :] = c_ref[...]
    else:
        c_ref, dw_ref, bias_ref, lng_ref, lnb_ref, out_ref, ext_ref = refs
        i = pl.program_id(1)
        t0 = pl.multiple_of(i * tc, 8)
        prev = pl.multiple_of(jnp.maximum(t0 - CONV_HIST, 0), 8)
        hist = c_ref[pl.ds(prev, CONV_HIST), :]
        ext_ref[0:CONV_HIST, :] = jnp.where(i > 0, hist, 0.0)
        ext_ref[CONV_HIST:, :] = c_ref[pl.ds(t0, tc), :]
    off = CONV_HIST - (CONV_WIDTH - 1)
    acc = ext_ref[pl.ds(off, tc), :] * dw_ref[0:1, :]
    for k in range(1, CONV_WIDTH):
        acc = acc + ext_ref[pl.ds(off + k, tc), :] * dw_ref[k:k + 1, :]
    y = acc + bias_ref[...]
    mu = jnp.mean(y, axis=-1, keepdims=True)
    yc = y - mu
    var = jnp.mean(yc * yc, axis=-1, keepdims=True)
    y = yc * lax.rsqrt(var + NORM_EPS) * lng_ref[...] + lnb_ref[...]
    out_ref[...] = y * _sigmoid(y)


def _conv_prompt(c, dw, bias, lng, lnb, batch, seq, tc):
    assert seq % tc == 0 and tc % 8 == 0
    nt = seq // tc
    vec = lambda n: pl.BlockSpec((1, n), lambda b, i: (0, 0))
    return pl.pallas_call(
        functools.partial(_conv_kernel, tc, False),
        grid=(batch, nt),
        in_specs=[pl.BlockSpec((seq, GROUP_W), lambda b, i: (b, 0)),
                  pl.BlockSpec((CONV_WIDTH, GROUP_W), lambda b, i: (0, 0)), vec(GROUP_W), vec(GROUP_W), vec(GROUP_W)],
        out_specs=pl.BlockSpec((tc, GROUP_W), lambda b, i, nt=nt: (b * nt + i, 0)),
        out_shape=jax.ShapeDtypeStruct((batch * seq, GROUP_W), F32),
        scratch_shapes=[pltpu.VMEM((tc + CONV_HIST, GROUP_W), F32)],
        compiler_params=_params("parallel", "arbitrary"),
        name="conv_prompt",
    )(c, dw, bias, lng, lnb)


def _conv_sample(c, hist, dw, bias, lng, lnb, dec_batch, ns):
    vec = lambda n: pl.BlockSpec((1, n), lambda b: (0, 0))
    return pl.pallas_call(
        functools.partial(_conv_kernel, ns, True),
        grid=(dec_batch,),
        in_specs=[pl.BlockSpec((ns, GROUP_W), lambda b: (b, 0)),
                  pl.BlockSpec((1, CONV_HIST, GROUP_W), lambda b: (b, 0, 0)),
                  pl.BlockSpec((CONV_WIDTH, GROUP_W), lambda b: (0, 0)), vec(GROUP_W), vec(GROUP_W), vec(GROUP_W)],
        out_specs=pl.BlockSpec((ns, GROUP_W), lambda b: (b, 0)),
        out_shape=jax.ShapeDtypeStruct((dec_batch * ns, GROUP_W), F32),
        scratch_shapes=[pltpu.VMEM((ns + CONV_HIST, GROUP_W), F32)],
        compiler_params=_params("parallel"),
        name="conv_sample",
    )(c, hist, dw, bias, lng, lnb)


def _merge_kernel(x_ref, attn_ref, u_ref, ga_ref, gb_ref, wa_ref, wc_ref, bc_ref, wo_ref, out_ref):
    y_a = jnp.dot(attn_ref[...].astype(BF16), wa_ref[...], preferred_element_type=F32)
    y_b = jnp.dot(u_ref[...].astype(BF16), wc_ref[...], preferred_element_type=F32) + bc_ref[...]
    y = (ga_ref[...] * y_a + gb_ref[...] * y_b).astype(BF16)
    out_ref[...] = x_ref[...] + jnp.dot(y, wo_ref[...], preferred_element_type=F32)


def _merge(x, attn, u, ga, gb, wa, wc, bc, wo, *, tm):
    t, d_model = x.shape
    assert t % tm == 0
    row = lambda i: (i, 0)
    const = lambda i: (0, 0)
    return pl.pallas_call(
        _merge_kernel,
        grid=(t // tm,),
        in_specs=[pl.BlockSpec((tm, d_model), row), pl.BlockSpec((tm, GROUP_W), row), pl.BlockSpec((tm, GROUP_W), row),
                  pl.BlockSpec((tm, d_model), row), pl.BlockSpec((tm, d_model), row),
                  pl.BlockSpec((GROUP_W, d_model), const), pl.BlockSpec((GROUP_W, d_model), const),
                  pl.BlockSpec((1, d_model), const), pl.BlockSpec((d_model, d_model), const)],
        out_specs=pl.BlockSpec((tm, d_model), row),
        out_shape=jax.ShapeDtypeStruct((t, d_model), F32),
        compiler_params=_params("parallel"),
        name="merge",
    )(x, attn, u, ga, gb, wa, wc, bc, wo)


_NO_RANK = 64.0


def _top_values(s, out_ref, n, want_rank=False):
    cur = s
    rank = jnp.full(s.shape, _NO_RANK, F32) if want_rank else None
    for j in range(n):
        m = jnp.max(cur, axis=0, keepdims=True)
        out_ref[j:j + 1, :] = m
        hit = cur == m
        if want_rank:
            rank = jnp.where(hit, float(j), rank)
        if j + 1 < n:
            cur = jnp.where(hit, -jnp.inf, cur)
    return rank


def _split_bf16(x):
    hi = x.astype(BF16)
    return hi, (x - hi.astype(F32)).astype(BF16)


def _dot_bf16x3(a_hi, a_lo, b_hi, b_lo):
    d = functools.partial(jnp.dot, preferred_element_type=F32)
    return d(a_hi, b_hi) + (d(a_hi, b_lo) + d(a_lo, b_hi))


_CAND = [(i, PEER_TOPK // (i + 1)) for i in range(PEER_TOPK)]
_CAND_ROWS = 64
PEER_SLAB = 256
GATE_ROWS = 16


def _peer_kernel(tp, ech, lb, n_chunks, x_ref, g_ref, wqh_ref, wql_ref, kh_ref, kl_ref, u_ref, vt_ref, out_ref,
                 ht_ref, htl_ref, cnt_ref, e1_ref, sa_ref, sb_ref, rank_ref, e2_ref, at_ref, gt_ref, acc_ref,
                 a_ref, b_ref, cand_ref, top_ref):
    c = pl.program_id(1)
    n_lb = tp // lb

    @pl.when(c == 0)
    def _route():
        h2 = _rmsnorm(x_ref[...], g_ref[...])
        ht = h2.T
        ht_hi, ht_lo = _split_bf16(ht)
        ht_ref[...] = ht_hi
        htl_ref[...] = ht_lo
        acc_ref[...] = jnp.zeros(acc_ref.shape, F32)
        gt_ref[1] = jnp.zeros(gt_ref.shape[1:], BF16)

        def scores(h, s_ref):
            for half in range(2):
                k = 2 * h + half
                rows = pl.ds(pl.multiple_of(k * N_KEYS, N_KEYS), N_KEYS)
                q = _dot_bf16x3(wqh_ref[rows, :], wql_ref[rows, :], ht_ref[...], htl_ref[...])
                q_hi, q_lo = _split_bf16(q)
                s_ref[half] = _dot_bf16x3(kh_ref[k], kl_ref[k], q_hi, q_lo)

        def select(h, s_ref):
            for j in range(n_lb):
                cols = slice(j * lb, (j + 1) * lb)
                s1 = s_ref[0, :, cols]
                s2 = s_ref[1, :, cols]
                _top_values(s1, a_ref, PEER_TOPK)
                rank2 = _top_values(s2, b_ref, PEER_TOPK, want_rank=True)
                cand_ref[...] = jnp.full(cand_ref.shape, -jnp.inf, F32)
                row = 0
                for (i, n) in _CAND:
                    cand_ref[row:row + n, :] = a_ref[i:i + 1, :] + b_ref[0:n, :]
                    row += n
                _top_values(cand_ref[...], top_ref, PEER_TOPK)
                top = top_ref[...]
                tau = top[PEER_TOPK - 1:PEER_TOPK, :]
                z = jnp.sum(jnp.exp(top - top[0:1, :]), axis=0, keepdims=True)
                a0, a15 = a_ref[0:1, :], a_ref[PEER_TOPK - 1:PEER_TOPK, :]
                b0 = b_ref[0:1, :]
                cnt = jnp.zeros(s1.shape, F32)
                for r in range(PEER_TOPK):
                    cnt = cnt + jnp.where(s1 + b_ref[r:r + 1, :] >= tau, 1.0, 0.0)
                cnt_ref[h, :, cols] = jnp.where(s1 >= a15, cnt, 0.0)
                e1_ref[h, :, cols] = jnp.exp(s1 - a0)
                rank_ref[h, :, cols] = rank2.astype(BF16)
                e2_ref[h, :, cols] = (jnp.exp(s2 - b0) / z).astype(BF16)

        scores(0, sa_ref)

        def head_pair(hp, carry):
            h = 2 * hp
            scores(h + 1, sb_ref)
            select(h, sa_ref)
            scores(jnp.minimum(h + 2, PEER_HEADS - 1), sa_ref)
            select(h + 1, sb_ref)
            return carry

        lax.fori_loop(0, PEER_HEADS // 2, head_pair, 0)

    n_slabs = ech // PEER_SLAB
    d_part = acc_ref.shape[0] // n_slabs

    cur = c % 2
    prev = 1 - cur

    def output_part(p):
        rows = slice(p * d_part, (p + 1) * d_part)
        acc_ref[rows, :] += jnp.dot(vt_ref[0, rows, :], gt_ref[prev], preferred_element_type=F32)

    def pre_activations(si):
        slab = slice(si * PEER_SLAB, (si + 1) * PEER_SLAB)
        at_ref[slab, :] = jnp.dot(u_ref[slab, :], ht_ref[...], preferred_element_type=F32)

    def gate_slab(si):
        i1_rows = pl.ds(pl.multiple_of(c * (ech // N_KEYS), 8), ech // N_KEYS)
        s0 = si * PEER_SLAB
        for i1l in range(s0 // N_KEYS, (s0 + PEER_SLAB) // N_KEYS):
            rows = slice(i1l * N_KEYS, (i1l + 1) * N_KEYS)
            for j in range(n_lb):
                cols = slice(j * lb, (j + 1) * lb)
                w = jnp.zeros((N_KEYS, lb), BF16)
                for h in range(PEER_HEADS):
                    cnt_b = jnp.broadcast_to(cnt_ref[h, i1_rows, cols][i1l:i1l + 1], (N_KEYS, lb)).astype(BF16)
                    e1_b = jnp.broadcast_to(e1_ref[h, i1_rows, cols][i1l:i1l + 1], (N_KEYS, lb)).astype(BF16)
                    keep = rank_ref[h, :, cols] < cnt_b
                    w = w + jnp.where(keep, e2_ref[h, :, cols] * e1_b, jnp.zeros((), BF16))
                a = at_ref[rows, cols]
                act = 0.5 * a * (1.0 + lax.erf(a * (0.5 ** 0.5)))
                gt_ref[cur, rows, cols] = w * act.astype(BF16)

    @pl.when(c < n_chunks)
    def _chunk():
        pre_activations(0)
        for si in range(n_slabs):
            if si + 1 < n_slabs:
                pre_activations(si + 1)
            output_part(si)
            gate_slab(si)

    @pl.when(c == n_chunks)
    def _drain():
        for p in range(n_slabs):
            output_part(p)
        out_ref[...] = x_ref[...] + acc_ref[...].T


PEER_CHUNK = 8 * N_KEYS
PEER_FLAGS = None


def _peer(x, g, wq_t, keys, u_bf16, vt_bf16, *, tp, lb=LANES):
    t, d_model = x.shape
    n_exp = u_bf16.shape[0]
    ech = PEER_CHUNK
    n_chunks = n_exp // ech
    assert t % tp == 0 and n_exp % ech == 0 and tp % lb == 0 and n_exp == N_KEYS * N_KEYS
    assert vt_bf16.shape == (n_chunks, d_model, ech)
    q_w = wq_t[0].shape[0]
    assert q_w == PEER_HEADS * 2 * N_KEYS and ech % PEER_SLAB == 0 and d_model % (ech // PEER_SLAB) == 0
    meta = lambda dt: pltpu.VMEM((PEER_HEADS, N_KEYS, tp), dt)
    wq_spec = pl.BlockSpec((q_w, d_model), lambda i, c: (0, 0), pipeline_mode=pl.Buffered(1))
    keys_spec = pl.BlockSpec((2 * PEER_HEADS, N_KEYS, N_KEYS), lambda i, c: (0, 0, 0), pipeline_mode=pl.Buffered(1))
    return pl.pallas_call(
        functools.partial(_peer_kernel, tp, ech, lb, n_chunks),
        grid=(t // tp, n_chunks + 1),
        in_specs=[pl.BlockSpec((tp, d_model), lambda i, c: (i, 0)),
                  pl.BlockSpec((1, d_model), lambda i, c: (0, 0)),
                  wq_spec, wq_spec, keys_spec, keys_spec,
                  pl.BlockSpec((ech, d_model), lambda i, c: (jnp.minimum(c, n_chunks - 1), 0)),
                  pl.BlockSpec((1, d_model, ech), lambda i, c: (jnp.maximum(c - 1, 0), 0, 0))],
        out_specs=pl.BlockSpec((tp, d_model), lambda i, c: (i, 0)),
        out_shape=jax.ShapeDtypeStruct((t, d_model), F32),
        scratch_shapes=[pltpu.VMEM((d_model, tp), BF16),
                        pltpu.VMEM((d_model, tp), BF16),
                        meta(F32), meta(F32),
                        pltpu.VMEM((2, N_KEYS, tp), F32), pltpu.VMEM((2, N_KEYS, tp), F32),
                        meta(BF16), meta(BF16),
                        pltpu.VMEM((ech, tp), F32),
                        pltpu.VMEM((2, ech, tp), BF16),
                        pltpu.VMEM((d_model, tp), F32),
                        pltpu.VMEM((PEER_TOPK, lb), F32), pltpu.VMEM((PEER_TOPK, lb), F32),
                        pltpu.VMEM((_CAND_ROWS, lb), F32), pltpu.VMEM((PEER_TOPK, lb), F32)],
        compiler_params=_params("parallel", "arbitrary", flags=PEER_FLAGS),
        name="peer",
    )(x, g, wq_t[0], wq_t[1], keys[0], keys[1], u_bf16, vt_bf16)


def _final_norm_kernel(x_ref, g_ref, out_ref):
    out_ref[...] = _rmsnorm(x_ref[...], g_ref[...])


def _final_norm(x, g, *, tm):
    t, d_model = x.shape
    return pl.pallas_call(
        _final_norm_kernel,
        grid=(t // tm,),
        in_specs=[pl.BlockSpec((tm, d_model), lambda i: (i, 0)), pl.BlockSpec((1, d_model), lambda i: (0, 0))],
        out_specs=pl.BlockSpec((tm, d_model), lambda i: (i, 0)),
        out_shape=jax.ShapeDtypeStruct((t, d_model), F32),
        compiler_params=_params("parallel"),
        name="final_norm",
    )(x, g)


def _time_last(buf):
    return buf.transpose(0, 1, 3, 4, 5, 2).reshape(buf.shape[0], buf.shape[1], 2, GROUP_W, buf.shape[2])


def _time_first(buf):
    depth, batch, _, _, time = buf.shape
    return buf.reshape(depth, batch, 2, HEADS_PER_GROUP, HEAD_DIM, time).transpose(0, 1, 5, 2, 3, 4)


def kernel(x_prompt, x_sample, cache_kv_w128, cache_kv_w512, cache_kv_w2048, state_conv, norm_mix_g, norm_ffn_g, w_in, dw_kernel, dw_bias, conv_ln_g, conv_ln_b, w_conv_out, b_conv_out, w_attn_out, w_out, w_peer_q, peer_sub_keys, peer_u, peer_v, norm_final_g):
    batch, seq, d_model = x_prompt.shape
    dec_batch, ns, _ = x_sample.shape
    depth = w_in.shape[0]
    tp_tokens, ts_tokens = batch * seq, dec_batch * ns
    tm_p = 512
    cache_in = (cache_kv_w128, cache_kv_w512, cache_kv_w2048)
    for c, (window, _) in zip(cache_in, ATTN_GROUPS):
        assert c.shape[2] == window
    caches = [_time_last(c) for c in cache_in]
    assert state_conv.shape[2] == CONV_WIDTH - 1
    hist_pad = CONV_HIST - (CONV_WIDTH - 1)

    xp = x_prompt.reshape(tp_tokens, d_model)
    xs = x_sample.reshape(ts_tokens, d_model)
    tails = [jnp.zeros((depth, batch, 2, GROUP_W, min(window, seq)), F32) for (window, _) in ATTN_GROUPS]
    new_caches = [jnp.zeros(c.shape, F32) for c in caches]
    conv_p, conv_s = [], []
    vec = lambda a: a.reshape(1, -1)
    for l in range(depth):
        w_in_b = w_in[l].astype(BF16)
        wa_b, wc_b, wo_b = w_attn_out[l].astype(BF16), w_conv_out[l].astype(BF16), w_out[l].astype(BF16)
        wq_t = _split_bf16(w_peer_q[l].T)
        keys = _split_bf16(peer_sub_keys[l].reshape(2 * PEER_HEADS, N_KEYS, -1))
        u_b = peer_u[l].astype(BF16)
        vt_b = peer_v[l].reshape(-1, PEER_CHUNK, d_model).transpose(0, 2, 1).astype(BF16)
        conv_w = (dw_kernel[l], vec(dw_bias[l]), vec(conv_ln_g[l]), vec(conv_ln_b[l]))
        mix_g, ffn_g = vec(norm_mix_g[l]), vec(norm_ffn_g[l])

        outs = _in_proj(xp, mix_g, w_in_b, tm=tm_p, class_major=True, batch=batch, seq=seq, layer=l, tails=tails)
        c, ga, gb = outs[:3]
        tails = list(outs[12:])
        attn = _attn_prompt(outs[3:12], batch, seq).reshape(tp_tokens, GROUP_W)
        u = _conv_prompt(c, *conv_w, batch, seq, 512)
        xp = _merge(xp, attn, u, ga, gb, wa_b, wc_b, vec(b_conv_out[l]), wo_b, tm=tm_p)
        xp = _peer(xp, ffn_g, wq_t, keys, u_b, vt_b, tp=512)
        conv_p.append(c.reshape(batch, seq, GROUP_W)[:, seq - (CONV_WIDTH - 1):])

        qkv, c, ga, gb = _in_proj(xs, mix_g, w_in_b, tm=ts_tokens, class_major=False)
        attn, *new_caches = _attn_sample(qkv, caches, new_caches, l, dec_batch, ns)
        hist = jnp.pad(state_conv[l], ((0, 0), (hist_pad, 0), (0, 0)))
        u = _conv_sample(c, hist, *conv_w, dec_batch, ns)
        xs = _merge(xs, attn, u, ga, gb, wa_b, wc_b, vec(b_conv_out[l]), wo_b, tm=ts_tokens)
        xs = _peer(xs, ffn_g, wq_t, keys, u_b, vt_b, tp=ts_tokens)
        full = jnp.concatenate([state_conv[l], c.reshape(dec_batch, ns, GROUP_W)], axis=1)
        conv_s.append(full[:, full.shape[1] - (CONV_WIDTH - 1):])

    y_prompt = _final_norm(xp, vec(norm_final_g), tm=tm_p).reshape(batch, seq, d_model)
    y_sample = _final_norm(xs, vec(norm_final_g), tm=ts_tokens).reshape(dec_batch, ns, d_model)
    return (y_prompt, y_sample,
            _time_first(tails[0]), _time_first(tails[1]), _time_first(tails[2]), jnp.stack(conv_p),
            _time_first(new_caches[0]), _time_first(new_caches[1]), _time_first(new_caches[2]), jnp.stack(conv_s))
```

```python
import functools

import jax
import jax.numpy as jnp
from jax import lax
from jax.experimental import pallas as pl
from jax.experimental.pallas import tpu as pltpu

F32 = jnp.float32
BF16 = jnp.bfloat16

HEAD_DIM = 64
HEADS_PER_GROUP = 8
ATTN_GROUPS = ((128, 1), (512, 4), (2048, 16))
N_GROUPS = len(ATTN_GROUPS)
GROUP_W = HEADS_PER_GROUP * HEAD_DIM
QKV_W = N_GROUPS * GROUP_W
ATTN_BLOCK = 128
ATTN_SCALE = HEAD_DIM ** -0.5
CONV_WIDTH = 31
CONV_HIST = 32
N_KEYS = 128
PEER_HEADS = 8
PEER_TOPK = 16
NORM_EPS = 1e-6
NEG_INF = -1e30

VMEM_LIMIT_BYTES = 56 * 1024 * 1024
LANES = 128
HEAD_HALF = 4
ATTN_UNROLL = 4
HALF_W = HEAD_HALF * HEAD_DIM

NT_DIMS = (((1,), (1,)), ((), ()))


def _params(*sem, vmem_limit_bytes=VMEM_LIMIT_BYTES):
    return pltpu.CompilerParams(dimension_semantics=sem, vmem_limit_bytes=vmem_limit_bytes)


def _rmsnorm(x, g):
    return x * lax.rsqrt(jnp.mean(x * x, axis=-1, keepdims=True) + NORM_EPS) * g


def _sigmoid(x):
    return 1.0 / (1.0 + jnp.exp(-x))


def _in_proj_kernel(class_major, tm, seq, x_ref, g_ref, w_ref, *refs):
    if class_major:
        c_ref, ga_ref, gb_ref = refs[3:6]
        cm_refs = refs[6:15]
        tail_refs = refs[15:18]
        scr_ref = refs[18]
        tps = seq // tm
        last_tile = pl.program_id(0) % tps == tps - 1
    else:
        qkv_ref, c_ref, ga_ref, gb_ref = refs[:4]
    h = _rmsnorm(x_ref[...], g_ref[...]).astype(BF16)
    glu_a = None
    for j in range(w_ref.shape[1] // GROUP_W):
        res = jnp.dot(h, w_ref[:, j * GROUP_W:(j + 1) * GROUP_W], preferred_element_type=F32)
        if j < 9:
            kind, g = divmod(j, N_GROUPS)
            if class_major:
                if kind > 0:
                    keep = tail_refs[g].shape[-1]
                    if min(ATTN_GROUPS[g][0], seq) == seq:
                        tail_refs[g][0, 0, kind - 1] = res.T
                    else:
                        @pl.when(last_tile)
                        def _(res=res, g=g, kind=kind, keep=keep):
                            tail_refs[g][0, 0, kind - 1] = res[tm - keep:, :].T
                val = res * ATTN_SCALE if kind == 0 else res
                d = ATTN_GROUPS[g][1]
                cm = cm_refs[kind * N_GROUPS + g]
                if d == 1:
                    cm[0, 0] = val.astype(BF16)
                else:
                    for t in range(GROUP_W // LANES):
                        scr_ref[t] = val[:, t * LANES:(t + 1) * LANES]
                    for r in range(d):
                        cm[0, r] = jnp.concatenate(
                            [scr_ref[t, pl.ds(r, tm // d, stride=d), :] for t in range(GROUP_W // LANES)], axis=1).astype(BF16)
            else:
                qkv_ref[:, j * GROUP_W:(j + 1) * GROUP_W] = res
        elif j == 9:
            glu_a = res
        elif j == 10:
            c_ref[...] = glu_a * _sigmoid(res)
        elif j < 13:
            ga_ref[:, (j - 11) * GROUP_W:(j - 10) * GROUP_W] = _sigmoid(res)
        else:
            gb_ref[:, (j - 13) * GROUP_W:(j - 12) * GROUP_W] = _sigmoid(res)


def _in_proj(x, g, w_bf16, *, tm, class_major, batch=None, seq=None, layer=None, tails=None):
    t, d_model = x.shape
    in_w = w_bf16.shape[1]
    assert t % tm == 0 and in_w == 3 * QKV_W + 2 * GROUP_W + 2 * d_model and d_model == 2 * GROUP_W
    nt = t // tm
    row = lambda i: (i, 0)
    in_specs = [
        pl.BlockSpec((tm, d_model), row),
        pl.BlockSpec((1, d_model), lambda i: (0, 0)),
        pl.BlockSpec((d_model, in_w), lambda i: (0, 0), pipeline_mode=pl.Buffered(1)),
    ]
    common_shapes = [
        jax.ShapeDtypeStruct((t, GROUP_W), F32),
        jax.ShapeDtypeStruct((t, d_model), F32),
        jax.ShapeDtypeStruct((t, d_model), F32),
    ]
    common_specs = [pl.BlockSpec((tm, GROUP_W), row), pl.BlockSpec((tm, d_model), row), pl.BlockSpec((tm, d_model), row)]
    aliases = {}
    args = (x, g, w_bf16)
    if class_major:
        assert seq % tm == 0
        tps = seq // tm
        in_specs += [pl.BlockSpec(memory_space=pl.ANY)] * N_GROUPS
        args += tuple(tails)
        out_shape = list(common_shapes)
        out_specs = list(common_specs)
        for _kind in range(3):
            for (_, d) in ATTN_GROUPS:
                assert tm % (d * 16) == 0
                out_shape.append(jax.ShapeDtypeStruct((batch, d, seq // d, GROUP_W), BF16))
                out_specs.append(pl.BlockSpec((1, d, tm // d, GROUP_W), lambda i, tps=tps: (i // tps, 0, i % tps, 0)))
        for g_idx, ((window, _), tail) in enumerate(zip(ATTN_GROUPS, tails)):
            keep = min(window, seq)
            assert tail.shape[1:] == (batch, 2, GROUP_W, keep) and (keep == seq or (keep <= tm and keep % LANES == 0))
            aliases[3 + g_idx] = len(out_shape)
            out_shape.append(jax.ShapeDtypeStruct(tail.shape, F32))
            if keep == seq:
                out_specs.append(pl.BlockSpec((1, 1, 2, GROUP_W, tm), lambda i, tps=tps: (layer, i // tps, 0, 0, i % tps)))
            else:
                out_specs.append(pl.BlockSpec((1, 1, 2, GROUP_W, keep), lambda i, tps=tps: (layer, i // tps, 0, 0, 0)))
        scratch = [pltpu.VMEM((GROUP_W // LANES, tm, LANES), F32)]
    else:
        out_shape = [jax.ShapeDtypeStruct((t, 3 * QKV_W), F32)] + common_shapes
        out_specs = [pl.BlockSpec((tm, 3 * QKV_W), row)] + common_specs
        scratch = []
    return pl.pallas_call(
        functools.partial(_in_proj_kernel, class_major, tm, seq),
        grid=(nt,),
        in_specs=in_specs,
        out_specs=out_specs,
        out_shape=out_shape,
        scratch_shapes=scratch,
        input_output_aliases=aliases,
        compiler_params=_params("arbitrary"),
        name="in_proj_cm" if class_major else "in_proj_nat",
    )(*args)


def _attend_heads(qb, kb, vb, mask, n_heads):
    outs, lses = [], []
    for h in range(n_heads):
        sl = slice(h * HEAD_DIM, (h + 1) * HEAD_DIM)
        s = lax.dot_general(qb[:, sl], kb[:, sl], NT_DIMS, preferred_element_type=F32)
        s = jnp.where(mask, s, NEG_INF)
        m = jnp.max(s, axis=-1, keepdims=True)
        p = jnp.exp(s - m)
        den = jnp.sum(p, axis=-1, keepdims=True)
        o = jnp.dot(p.astype(BF16), vb[:, sl], preferred_element_type=F32) / den
        outs.append(o)
        lses.append(jnp.broadcast_to(m + jnp.log(den), o.shape))
    return jnp.concatenate(outs, axis=1), jnp.concatenate(lses, axis=1)


def _attn_prompt_kernel(seq, *refs):
    qkv_refs = refs[:9]
    out_ref = refs[9]
    ocm_ref, lcm_ref, m_ref, n_ref, d_ref = refs[10:]
    n_lt = HALF_W // LANES
    dist = (lax.broadcasted_iota(jnp.int32, (ATTN_BLOCK, 2 * ATTN_BLOCK), 0) + ATTN_BLOCK
            - lax.broadcasted_iota(jnp.int32, (ATTN_BLOCK, 2 * ATTN_BLOCK), 1))
    band_mask = (dist >= 0) & (dist <= ATTN_BLOCK)
    causal_mask = (lax.broadcasted_iota(jnp.int32, (ATTN_BLOCK, ATTN_BLOCK), 0)
                   >= lax.broadcasted_iota(jnp.int32, (ATTN_BLOCK, ATTN_BLOCK), 1))

    def put(ref, rows, val):
        for t in range(n_lt):
            ref[t, rows, :] = val[:, t * LANES:(t + 1) * LANES]

    for g, (window, d) in enumerate(ATTN_GROUPS):
        assert window // d == ATTN_BLOCK
        q_ref, k_ref, v_ref = qkv_refs[g], qkv_refs[N_GROUPS + g], qkv_refs[2 * N_GROUPS + g]
        sd = seq // d
        nb = sd // ATTN_BLOCK

        def block(r, i, q_ref=q_ref, k_ref=k_ref, v_ref=v_ref, sd=sd):
            if isinstance(i, int) and i == 0:
                q0 = 0
                o, l = _attend_heads(q_ref[0, r, 0:ATTN_BLOCK, :], k_ref[0, r, 0:ATTN_BLOCK, :],
                                     v_ref[0, r, 0:ATTN_BLOCK, :], causal_mask, HEAD_HALF)
            else:
                q0, k0 = i * ATTN_BLOCK, (i - 1) * ATTN_BLOCK
                if not isinstance(i, int):
                    q0, k0 = pl.multiple_of(q0, ATTN_BLOCK), pl.multiple_of(k0, ATTN_BLOCK)
                o, l = _attend_heads(q_ref[0, r, pl.ds(q0, ATTN_BLOCK), :], k_ref[0, r, pl.ds(k0, 2 * ATTN_BLOCK), :],
                                     v_ref[0, r, pl.ds(k0, 2 * ATTN_BLOCK), :], band_mask, HEAD_HALF)
            dst = r * sd + q0
            if not isinstance(dst, int):
                dst = pl.multiple_of(dst, ATTN_BLOCK)
            put(ocm_ref, pl.ds(dst, ATTN_BLOCK), o)
            put(lcm_ref, pl.ds(dst, ATTN_BLOCK), l)

        if nb == 1:
            assert d % ATTN_UNROLL == 0

            def classes(it, carry, block=block):
                for u in range(ATTN_UNROLL):
                    block(it * ATTN_UNROLL + u, 0)
                return carry

            lax.fori_loop(0, d // ATTN_UNROLL, classes, 0)
        else:
            assert nb % ATTN_UNROLL == 0

            def one_class(r, carry, block=block, nb=nb):
                for i in range(ATTN_UNROLL):
                    block(r, i)

                def later_blocks(it, c2):
                    for u in range(ATTN_UNROLL):
                        block(r, (it + 1) * ATTN_UNROLL + u)
                    return c2

                if nb > ATTN_UNROLL:
                    lax.fori_loop(0, nb // ATTN_UNROLL - 1, later_blocks, 0)
                return carry

            if d == 1:
                one_class(0, 0)
            else:
                lax.fori_loop(0, d, one_class, 0)

        if g == 0:
            assert d == 1
            m_ref[...] = lcm_ref[...]
            n_ref[...] = ocm_ref[...]
            d_ref[...] = jnp.ones(d_ref.shape, F32)
        else:
            for r in range(d):
                rows = pl.ds(r, sd, stride=d)
                for t in range(n_lt):
                    o_r = ocm_ref[t, r * sd:(r + 1) * sd, :]
                    l_r = lcm_ref[t, r * sd:(r + 1) * sd, :]
                    m_old = m_ref[t, rows, :]
                    m_new = jnp.maximum(m_old, l_r)
                    a = jnp.exp(m_old - m_new)
                    b = jnp.exp(l_r - m_new)
                    n_ref[t, rows, :] = n_ref[t, rows, :] * a + o_r * b
                    d_ref[t, rows, :] = d_ref[t, rows, :] * a + b
                    m_ref[t, rows, :] = m_new
    for t in range(n_lt):
        out_ref[0, :, t * LANES:(t + 1) * LANES] = (n_ref[t] / d_ref[t]).astype(BF16)


def _attn_prompt(cm, batch, seq):
    in_specs = []
    for _kind in range(3):
        for (_, d) in ATTN_GROUPS:
            assert seq % (d * ATTN_BLOCK) == 0
            in_specs.append(pl.BlockSpec((1, d, seq // d, HALF_W), lambda b, hh: (b, 0, 0, hh)))
    return pl.pallas_call(
        functools.partial(_attn_prompt_kernel, seq),
        grid=(batch, GROUP_W // HALF_W),
        in_specs=in_specs,
        out_specs=pl.BlockSpec((1, seq, HALF_W), lambda b, hh: (b, 0, hh)),
        out_shape=jax.ShapeDtypeStruct((batch, seq, GROUP_W), BF16),
        scratch_shapes=[pltpu.VMEM((HALF_W // LANES, seq, LANES), F32) for _ in range(5)],
        compiler_params=_params("parallel", "parallel"),
        name="attn_prompt",
    )(*cm)


def _attn_sample_kernel(ns, qkv_ref, c0_ref, c1_ref, c2_ref, _n0, _n1, _n2, out_ref, new0_ref, new1_ref, new2_ref):
    cache_refs = (c0_ref, c1_ref, c2_ref)
    new_refs = (new0_ref, new1_ref, new2_ref)
    qkv = qkv_ref[...]
    row_pad = jnp.zeros((LANES - ns, GROUP_W), F32)
    lane = lax.broadcasted_iota(jnp.int32, (HEADS_PER_GROUP, GROUP_W), 1)
    hrow = lax.broadcasted_iota(jnp.int32, (HEADS_PER_GROUP, GROUP_W), 0)
    head_mask = (lane // HEAD_DIM == hrow).astype(F32)
    nr = ns * HEADS_PER_GROUP
    n_new = 16
    assert ns <= n_new
    zpad = jnp.zeros((n_new - ns, GROUP_W), F32)
    outs, lses = [], []
    for g, (window, d) in enumerate(ATTN_GROUPS):
        assert window % d == 0 and d & (d - 1) == 0 and cache_refs[g].shape[2:] == (2, GROUP_W, window)
        k_t = cache_refs[g][0, 0, 0].astype(BF16)
        v_t = cache_refs[g][0, 0, 1].astype(BF16)
        q = qkv[:, g * GROUP_W:(g + 1) * GROUP_W] * ATTN_SCALE
        k_new32 = qkv[:, QKV_W + g * GROUP_W:QKV_W + (g + 1) * GROUP_W]
        v_new32 = qkv[:, 2 * QKV_W + g * GROUP_W:2 * QKV_W + (g + 1) * GROUP_W]
        for kv, new32 in enumerate((k_new32, v_new32)):
            new_t = jnp.concatenate([new32, row_pad], axis=0).T
            new_refs[g][0, 0, kv] = jnp.concatenate([cache_refs[g][0, 0, kv][:, ns:], new_t[:, :ns]], axis=1)
        k_new = jnp.concatenate([k_new32, zpad], axis=0).astype(BF16)
        v_new = jnp.concatenate([v_new32, zpad], axis=0).astype(BF16)
        q_exp = jnp.concatenate([q[n:n + 1] * head_mask for n in range(ns)], axis=0).astype(BF16)
        s_c = jnp.dot(q_exp, k_t, preferred_element_type=F32)
        s_n = lax.dot_general(q_exp, k_new, NT_DIMS, preferred_element_type=F32)
        delta_c = (lax.broadcasted_iota(jnp.int32, (nr, window), 1)
                   - lax.broadcasted_iota(jnp.int32, (nr, window), 0) // HEADS_PER_GROUP)
        delta_n = (lax.broadcasted_iota(jnp.int32, (nr, n_new), 0) // HEADS_PER_GROUP
                   - lax.broadcasted_iota(jnp.int32, (nr, n_new), 1))
        s_c = jnp.where((delta_c >= 0) & ((delta_c & (d - 1)) == 0), s_c, NEG_INF)
        s_n = jnp.where((delta_n >= 0) & ((delta_n & (d - 1)) == 0), s_n, NEG_INF)
        m = jnp.maximum(jnp.max(s_c, axis=-1, keepdims=True), jnp.max(s_n, axis=-1, keepdims=True))
        p_c = jnp.exp(s_c - m)
        p_n = jnp.exp(s_n - m)
        den = jnp.sum(p_c, axis=-1, keepdims=True) + jnp.sum(p_n, axis=-1, keepdims=True)
        pv = (lax.dot_general(p_c.astype(BF16), v_t, NT_DIMS, preferred_element_type=F32)
              + jnp.dot(p_n.astype(BF16), v_new, preferred_element_type=F32)) / den
        lse = jnp.broadcast_to(m + jnp.log(den), pv.shape)
        o_rows, l_rows = [], []
        for n in range(ns):
            blk = slice(n * HEADS_PER_GROUP, (n + 1) * HEADS_PER_GROUP)
            o_rows.append(jnp.sum(pv[blk] * head_mask, axis=0, keepdims=True))
            l_rows.append(jnp.sum(lse[blk] * head_mask, axis=0, keepdims=True))
        outs.append(jnp.concatenate(o_rows, axis=0))
        lses.append(jnp.concatenate(l_rows, axis=0))
    m = jnp.maximum(jnp.maximum(lses[0], lses[1]), lses[2])
    es = [jnp.exp(l - m) for l in lses]
    num = es[0] * outs[0] + es[1] * outs[1] + es[2] * outs[2]
    out_ref[...] = num / (es[0] + es[1] + es[2])


def _attn_sample(qkv, caches, new_caches, layer, dec_batch, ns):
    assert ns % 8 == 0
    in_specs = [pl.BlockSpec((ns, 3 * QKV_W), lambda b: (b, 0))]
    out_specs = [pl.BlockSpec((ns, GROUP_W), lambda b: (b, 0))]
    out_shape = [jax.ShapeDtypeStruct((dec_batch * ns, GROUP_W), F32)]
    for c in caches:
        spec = pl.BlockSpec((1, 1) + c.shape[2:], lambda b, layer=layer: (layer, b, 0, 0, 0))
        in_specs.append(spec)
        out_specs.append(spec)
        out_shape.append(jax.ShapeDtypeStruct(c.shape, F32))
    in_specs += [pl.BlockSpec(memory_space=pl.ANY)] * len(new_caches)
    n_in = 1 + len(caches)
    return pl.pallas_call(
        functools.partial(_attn_sample_kernel, ns),
        grid=(dec_batch,),
        in_specs=in_specs,
        out_specs=out_specs,
        out_shape=out_shape,
        input_output_aliases={n_in + g: 1 + g for g in range(len(new_caches))},
        compiler_params=_params("arbitrary"),
        name="attn_sample",
    )(qkv, *caches, *new_caches)


def _conv_kernel(tc, has_hist, *refs):
    if has_hist:
        c_ref, hist_ref, dw_ref, bias_ref, lng_ref, lnb_ref, out_ref, ext_ref = refs
        ext_ref[0:CONV_HIST, :] = hist_ref[0]
        ext_ref[CONV_HIST:, :] = c_ref[...]
    else:
        c_ref, dw_ref, bias_ref, lng_ref, lnb_ref, out_ref, ext_ref = refs
        i = pl.program_id(1)
        t0 = pl.multiple_of(i * tc, 8)
        prev = pl.multiple_of(jnp.maximum(t0 - CONV_HIST, 0), 8)
        hist = c_ref[pl.ds(prev, CONV_HIST), :]
        ext_ref[0:CONV_HIST, :] = jnp.where(i > 0, hist, 0.0)
        ext_ref[CONV_HIST:, :] = c_ref[pl.ds(t0, tc), :]
    off = CONV_HIST - (CONV_WIDTH - 1)
    acc = ext_ref[pl.ds(off, tc), :] * dw_ref[0:1, :]
    for k in range(1, CONV_WIDTH):
        acc = acc + ext_ref[pl.ds(off + k, tc), :] * dw_ref[k:k + 1, :]
    y = acc + bias_ref[...]
    mu = jnp.mean(y, axis=-1, keepdims=True)
    yc = y - mu
    var = jnp.mean(yc * yc, axis=-1, keepdims=True)
    y = yc * lax.rsqrt(var + NORM_EPS) * lng_ref[...] + lnb_ref[...]
    out_ref[...] = y * _sigmoid(y)


def _conv_prompt(c, dw, bias, lng, lnb, batch, seq, tc):
    assert seq % tc == 0 and tc % 8 == 0
    nt = seq // tc
    vec = lambda n: pl.BlockSpec((1, n), lambda b, i: (0, 0))
    return pl.pallas_call(
        functools.partial(_conv_kernel, tc, False),
        grid=(batch, nt),
        in_specs=[pl.BlockSpec((seq, GROUP_W), lambda b, i: (b, 0)),
                  pl.BlockSpec((CONV_WIDTH, GROUP_W), lambda b, i: (0, 0)), vec(GROUP_W), vec(GROUP_W), vec(GROUP_W)],
        out_specs=pl.BlockSpec((tc, GROUP_W), lambda b, i, nt=nt: (b * nt + i, 0)),
        out_shape=jax.ShapeDtypeStruct((batch * seq, GROUP_W), F32),
        scratch_shapes=[pltpu.VMEM((tc + CONV_HIST, GROUP_W), F32)],
        compiler_params=_params("parallel", "arbitrary"),
        name="conv_prompt",
    )(c, dw, bias, lng, lnb)


def _conv_sample(c, hist, dw, bias, lng, lnb, dec_batch, ns):
    vec = lambda n: pl.BlockSpec((1, n), lambda b: (0, 0))
    return pl.pallas_call(
        functools.partial(_conv_kernel, ns, True),
        grid=(dec_batch,),
        in_specs=[pl.BlockSpec((ns, GROUP_W), lambda b: (b, 0)),
                  pl.BlockSpec((1, CONV_HIST, GROUP_W), lambda b: (b, 0, 0)),
                  pl.BlockSpec((CONV_WIDTH, GROUP_W), lambda b: (0, 0)), vec(GROUP_W), vec(GROUP_W), vec(GROUP_W)],
        out_specs=pl.BlockSpec((ns, GROUP_W), lambda b: (b, 0)),
        out_shape=jax.ShapeDtypeStruct((dec_batch * ns, GROUP_W), F32),
        scratch_shapes=[pltpu.VMEM((ns + CONV_HIST, GROUP_W), F32)],
        compiler_params=_params("parallel"),
        name="conv_sample",
    )(c, hist, dw, bias, lng, lnb)


def _merge_kernel(x_ref, attn_ref, u_ref, ga_ref, gb_ref, wa_ref, wc_ref, bc_ref, wo_ref, out_ref):
    y_a = jnp.dot(attn_ref[...].astype(BF16), wa_ref[...], preferred_element_type=F32)
    y_b = jnp.dot(u_ref[...].astype(BF16), wc_ref[...], preferred_element_type=F32) + bc_ref[...]
    y = (ga_ref[...] * y_a + gb_ref[...] * y_b).astype(BF16)
    out_ref[...] = x_ref[...] + jnp.dot(y, wo_ref[...], preferred_element_type=F32)


def _merge(x, attn, u, ga, gb, wa, wc, bc, wo, *, tm):
    t, d_model = x.shape
    assert t % tm == 0
    row = lambda i: (i, 0)
    const = lambda i: (0, 0)
    return pl.pallas_call(
        _merge_kernel,
        grid=(t // tm,),
        in_specs=[pl.BlockSpec((tm, d_model), row), pl.BlockSpec((tm, GROUP_W), row), pl.BlockSpec((tm, GROUP_W), row),
                  pl.BlockSpec((tm, d_model), row), pl.BlockSpec((tm, d_model), row),
                  pl.BlockSpec((GROUP_W, d_model), const), pl.BlockSpec((GROUP_W, d_model), const),
                  pl.BlockSpec((1, d_model), const), pl.BlockSpec((d_model, d_model), const)],
        out_specs=pl.BlockSpec((tm, d_model), row),
        out_shape=jax.ShapeDtypeStruct((t, d_model), F32),
        compiler_params=_params("parallel"),
        name="merge",
    )(x, attn, u, ga, gb, wa, wc, bc, wo)


_NO_RANK = 64.0


def _top_values(s, out_ref, n, want_rank=False):
    cur = s
    rank = jnp.full(s.shape, _NO_RANK, F32) if want_rank else None
    for j in range(n):
        m = jnp.max(cur, axis=0, keepdims=True)
        out_ref[j:j + 1, :] = m
        hit = cur == m
        if want_rank:
            rank = jnp.where(hit, float(j), rank)
        if j + 1 < n:
            cur = jnp.where(hit, -jnp.inf, cur)
    return rank


def _split_bf16(x):
    hi = x.astype(BF16)
    return hi, (x - hi.astype(F32)).astype(BF16)


def _dot_bf16x3(a_hi, a_lo, b_hi, b_lo):
    d = functools.partial(jnp.dot, preferred_element_type=F32)
    return d(a_hi, b_hi) + (d(a_hi, b_lo) + d(a_lo, b_hi))


_CAND = [(i, PEER_TOPK // (i + 1)) for i in range(PEER_TOPK)]
_CAND_ROWS = 64
PEER_SLAB = 256
GATE_ROWS = 16


def _peer_kernel(tp, ech, lb, n_chunks, x_ref, g_ref, wqh_ref, wql_ref, kh_ref, kl_ref, u_ref, vt_ref, out_ref,
                 ht_ref, htl_ref, cnt_ref, e1_ref, sa_ref, sb_ref, rank_ref, e2_ref, at_ref, gt_ref, acc_ref,
                 a_ref, b_ref, cand_ref, top_ref):
    c = pl.program_id(1)
    n_lb = tp // lb

    @pl.when(c == 0)
    def _route():
        for t0 in range(0, tp, TOKEN_BLOCK):
            tok = slice(t0, t0 + TOKEN_BLOCK)
            ht = _rmsnorm(x_ref[tok, :], g_ref[...]).T
            ht_hi, ht_lo = _split_bf16(ht)
            ht_ref[:, tok] = ht_hi
            htl_ref[:, tok] = ht_lo
        acc_ref[...] = jnp.zeros(acc_ref.shape, F32)
        gt_ref[1] = jnp.zeros(gt_ref.shape[1:], BF16)

        def scores(h, s_ref):
            for half in range(2):
                k = 2 * h + half
                rows = pl.ds(pl.multiple_of(k * N_KEYS, N_KEYS), N_KEYS)
                q = _dot_bf16x3(wqh_ref[rows, :], wql_ref[rows, :], ht_ref[...], htl_ref[...])
                q_hi, q_lo = _split_bf16(q)
                s_ref[half] = _dot_bf16x3(kh_ref[k], kl_ref[k], q_hi, q_lo)

        def select(h, s_ref):
            for j in range(n_lb):
                cols = slice(j * lb, (j + 1) * lb)
                s1 = s_ref[0, :, cols]
                s2 = s_ref[1, :, cols]
                _top_values(s1, a_ref, PEER_TOPK)
                rank2 = _top_values(s2, b_ref, PEER_TOPK, want_rank=True)
                cand_ref[...] = jnp.full(cand_ref.shape, -jnp.inf, F32)
                row = 0
                for (i, n) in _CAND:
                    cand_ref[row:row + n, :] = a_ref[i:i + 1, :] + b_ref[0:n, :]
                    row += n
                _top_values(cand_ref[...], top_ref, PEER_TOPK)
                top = top_ref[...]
                tau = top[PEER_TOPK - 1:PEER_TOPK, :]
                z = jnp.sum(jnp.exp(top - top[0:1, :]), axis=0, keepdims=True)
                a0, a15 = a_ref[0:1, :], a_ref[PEER_TOPK - 1:PEER_TOPK, :]
                b0 = b_ref[0:1, :]
                cnt = jnp.zeros(s1.shape, F32)
                for r in range(PEER_TOPK):
                    cnt = cnt + jnp.where(s1 + b_ref[r:r + 1, :] >= tau, 1.0, 0.0)
                cnt_ref[h, :, cols] = jnp.where(s1 >= a15, cnt, 0.0).astype(BF16)
                e1_ref[h, :, cols] = jnp.exp(s1 - a0).astype(BF16)
                rank_ref[h, :, cols] = rank2.astype(BF16)
                e2_ref[h, :, cols] = (jnp.exp(s2 - b0) / z).astype(BF16)

        scores(0, sa_ref)

        def head_pair(hp, carry):
            h = 2 * hp
            scores(h + 1, sb_ref)
            select(h, sa_ref)
            scores(jnp.minimum(h + 2, PEER_HEADS - 1), sa_ref)
            select(h + 1, sb_ref)
            return carry

        lax.fori_loop(0, PEER_HEADS // 2, head_pair, 0)

    n_slabs = ech // PEER_SLAB
    d_part = acc_ref.shape[0] // n_slabs

    cur = c % 2
    prev = 1 - cur

    def output_part(p):
        rows = slice(p * d_part, (p + 1) * d_part)
        acc_ref[rows, :] += jnp.dot(vt_ref[0, rows, :], gt_ref[prev], preferred_element_type=F32)

    def pre_activations(si):
        slab = slice(si * PEER_SLAB, (si + 1) * PEER_SLAB)
        at_ref[si % 2] = jnp.dot(u_ref[slab, :], ht_ref[...], preferred_element_type=F32)

    keys_per_chunk = ech // N_KEYS
    chunks_per_tile = META_TILE // keys_per_chunk

    def gate_slab(si, sub):
        i1_tile = pl.ds(pl.multiple_of((c // chunks_per_tile) * META_TILE, META_TILE), META_TILE)
        s0 = si * PEER_SLAB
        for i1l in range(s0 // N_KEYS, (s0 + PEER_SLAB) // N_KEYS):
            rows = slice(i1l * N_KEYS, (i1l + 1) * N_KEYS)
            r = sub * keys_per_chunk + i1l
            for j in range(n_lb):
                cols = slice(j * lb, (j + 1) * lb)
                w = jnp.zeros((N_KEYS, lb), BF16)
                for h in range(PEER_HEADS):
                    cnt_b = jnp.broadcast_to(cnt_ref[h, i1_tile, cols][r:r + 1], (N_KEYS, lb))
                    e1_b = jnp.broadcast_to(e1_ref[h, i1_tile, cols][r:r + 1], (N_KEYS, lb))
                    keep = rank_ref[h, :, cols] < cnt_b
                    w = w + jnp.where(keep, e2_ref[h, :, cols] * e1_b, jnp.zeros((), BF16))
                a = at_ref[si % 2, (i1l * N_KEYS - s0):(i1l * N_KEYS - s0) + N_KEYS, cols]
                act = 0.5 * a * (1.0 + lax.erf(a * (0.5 ** 0.5)))
                gt_ref[cur, rows, cols] = w * act.astype(BF16)

    for sub in range(chunks_per_tile):
        @pl.when((c < n_chunks) & (c % chunks_per_tile == sub))
        def _chunk(sub=sub):
            pre_activations(0)
            for si in range(n_slabs):
                if si + 1 < n_slabs:
                    pre_activations(si + 1)
                output_part(si)
                gate_slab(si, sub)

    @pl.when(c == n_chunks)
    def _drain():
        for p in range(n_slabs):
            output_part(p)
        for t0 in range(0, tp, TOKEN_BLOCK):
            tok = slice(t0, t0 + TOKEN_BLOCK)
            out_ref[tok, :] = x_ref[tok, :] + acc_ref[:, tok].T


PEER_CHUNK = 4 * N_KEYS
META_TILE = 16
PEER_VMEM_LIMIT_BYTES = 60000 * 1024
TOKEN_BLOCK = 128
PEER_TOKENS = 1024


def _peer(x, g, wq_t, keys, u_bf16, vt_bf16, *, tp, lb=LANES):
    t, d_model = x.shape
    n_exp = u_bf16.shape[0]
    ech = PEER_CHUNK
    n_chunks = n_exp // ech
    assert t % tp == 0 and n_exp % ech == 0 and tp % lb == 0 and n_exp == N_KEYS * N_KEYS
    assert vt_bf16.shape == (n_chunks, d_model, ech)
    q_w = wq_t[0].shape[0]
    assert q_w == PEER_HEADS * 2 * N_KEYS and ech % PEER_SLAB == 0 and d_model % (ech // PEER_SLAB) == 0
    meta = lambda dt: pltpu.VMEM((PEER_HEADS, N_KEYS, tp), dt)
    wq_spec = pl.BlockSpec((q_w, d_model), lambda i, c: (0, 0), pipeline_mode=pl.Buffered(1))
    keys_spec = pl.BlockSpec((2 * PEER_HEADS, N_KEYS, N_KEYS), lambda i, c: (0, 0, 0), pipeline_mode=pl.Buffered(1))
    return pl.pallas_call(
        functools.partial(_peer_kernel, tp, ech, lb, n_chunks),
        grid=(t // tp, n_chunks + 1),
        in_specs=[pl.BlockSpec((tp, d_model), lambda i, c: (i, 0), pipeline_mode=pl.Buffered(1)),
                  pl.BlockSpec((1, d_model), lambda i, c: (0, 0)),
                  wq_spec, wq_spec, keys_spec, keys_spec,
                  pl.BlockSpec((ech, d_model), lambda i, c: (jnp.minimum(c, n_chunks - 1), 0)),
                  pl.BlockSpec((1, d_model, ech), lambda i, c: (jnp.maximum(c - 1, 0), 0, 0))],
        out_specs=pl.BlockSpec((tp, d_model), lambda i, c: (i, 0), pipeline_mode=pl.Buffered(1)),
        out_shape=jax.ShapeDtypeStruct((t, d_model), F32),
        scratch_shapes=[pltpu.VMEM((d_model, tp), BF16),
                        pltpu.VMEM((d_model, tp), BF16),
                        meta(BF16), meta(BF16),
                        pltpu.VMEM((2, N_KEYS, tp), F32), pltpu.VMEM((2, N_KEYS, tp), F32),
                        meta(BF16), meta(BF16),
                        pltpu.VMEM((2, PEER_SLAB, tp), F32),
                        pltpu.VMEM((2, ech, tp), BF16),
                        pltpu.VMEM((d_model, tp), F32),
                        pltpu.VMEM((PEER_TOPK, lb), F32), pltpu.VMEM((PEER_TOPK, lb), F32),
                        pltpu.VMEM((_CAND_ROWS, lb), F32), pltpu.VMEM((PEER_TOPK, lb), F32)],
        compiler_params=_params("parallel", "arbitrary", vmem_limit_bytes=PEER_VMEM_LIMIT_BYTES),
        name="peer",
    )(x, g, wq_t[0], wq_t[1], keys[0], keys[1], u_bf16, vt_bf16)


def _final_norm_kernel(x_ref, g_ref, out_ref):
    out_ref[...] = _rmsnorm(x_ref[...], g_ref[...])


def _final_norm(x, g, *, tm):
    t, d_model = x.shape
    return pl.pallas_call(
        _final_norm_kernel,
        grid=(t // tm,),
        in_specs=[pl.BlockSpec((tm, d_model), lambda i: (i, 0)), pl.BlockSpec((1, d_model), lambda i: (0, 0))],
        out_specs=pl.BlockSpec((tm, d_model), lambda i: (i, 0)),
        out_shape=jax.ShapeDtypeStruct((t, d_model), F32),
        compiler_params=_params("parallel"),
        name="final_norm",
    )(x, g)


def _time_last(buf):
    return buf.transpose(0, 1, 3, 4, 5, 2).reshape(buf.shape[0], buf.shape[1], 2, GROUP_W, buf.shape[2])


def _time_first(buf):
    depth, batch, _, _, time = buf.shape
    return buf.reshape(depth, batch, 2, HEADS_PER_GROUP, HEAD_DIM, time).transpose(0, 1, 5, 2, 3, 4)


def kernel(x_prompt, x_sample, cache_kv_w128, cache_kv_w512, cache_kv_w2048, state_conv, norm_mix_g, norm_ffn_g, w_in, dw_kernel, dw_bias, conv_ln_g, conv_ln_b, w_conv_out, b_conv_out, w_attn_out, w_out, w_peer_q, peer_sub_keys, peer_u, peer_v, norm_final_g):
    batch, seq, d_model = x_prompt.shape
    dec_batch, ns, _ = x_sample.shape
    depth = w_in.shape[0]
    tp_tokens, ts_tokens = batch * seq, dec_batch * ns
    tm_p = 512
    cache_in = (cache_kv_w128, cache_kv_w512, cache_kv_w2048)
    for c, (window, _) in zip(cache_in, ATTN_GROUPS):
        assert c.shape[2] == window
    caches = [_time_last(c) for c in cache_in]
    assert state_conv.shape[2] == CONV_WIDTH - 1
    hist_pad = CONV_HIST - (CONV_WIDTH - 1)

    xp = x_prompt.reshape(tp_tokens, d_model)
    xs = x_sample.reshape(ts_tokens, d_model)
    tails = [jnp.zeros((depth, batch, 2, GROUP_W, min(window, seq)), F32) for (window, _) in ATTN_GROUPS]
    new_caches = [jnp.zeros(c.shape, F32) for c in caches]
    conv_p, conv_s = [], []
    vec = lambda a: a.reshape(1, -1)
    for l in range(depth):
        w_in_b = w_in[l].astype(BF16)
        wa_b, wc_b, wo_b = w_attn_out[l].astype(BF16), w_conv_out[l].astype(BF16), w_out[l].astype(BF16)
        wq_t = _split_bf16(w_peer_q[l].T)
        keys = _split_bf16(peer_sub_keys[l].reshape(2 * PEER_HEADS, N_KEYS, -1))
        u_b = peer_u[l].astype(BF16)
        vt_b = peer_v[l].reshape(-1, PEER_CHUNK, d_model).transpose(0, 2, 1).astype(BF16)
        conv_w = (dw_kernel[l], vec(dw_bias[l]), vec(conv_ln_g[l]), vec(conv_ln_b[l]))
        mix_g, ffn_g = vec(norm_mix_g[l]), vec(norm_ffn_g[l])

        outs = _in_proj(xp, mix_g, w_in_b, tm=tm_p, class_major=True, batch=batch, seq=seq, layer=l, tails=tails)
        c, ga, gb = outs[:3]
        tails = list(outs[12:])
        attn = _attn_prompt(outs[3:12], batch, seq).reshape(tp_tokens, GROUP_W)
        u = _conv_prompt(c, *conv_w, batch, seq, 512)
        xp = _merge(xp, attn, u, ga, gb, wa_b, wc_b, vec(b_conv_out[l]), wo_b, tm=tm_p)
        xp = _peer(xp, ffn_g, wq_t, keys, u_b, vt_b, tp=PEER_TOKENS)
        conv_p.append(c.reshape(batch, seq, GROUP_W)[:, seq - (CONV_WIDTH - 1):])

        qkv, c, ga, gb = _in_proj(xs, mix_g, w_in_b, tm=ts_tokens, class_major=False)
        attn, *new_caches = _attn_sample(qkv, caches, new_caches, l, dec_batch, ns)
        hist = jnp.pad(state_conv[l], ((0, 0), (hist_pad, 0), (0, 0)))
        u = _conv_sample(c, hist, *conv_w, dec_batch, ns)
        xs = _merge(xs, attn, u, ga, gb, wa_b, wc_b, vec(b_conv_out[l]), wo_b, tm=ts_tokens)
        xs = _peer(xs, ffn_g, wq_t, keys, u_b, vt_b, tp=ts_tokens)
        full = jnp.concatenate([state_conv[l], c.reshape(dec_batch, ns, GROUP_W)], axis=1)
        conv_s.append(full[:, full.shape[1] - (CONV_WIDTH - 1):])

    y_prompt = _final_norm(xp, vec(norm_final_g), tm=tm_p).reshape(batch, seq, d_model)
    y_sample = _final_norm(xs, vec(norm_final_g), tm=ts_tokens).reshape(dec_batch, ns, d_model)
    return (y_prompt, y_sample,
            _time_first(tails[0]), _time_first(tails[1]), _time_first(tails[2]), jnp.stack(conv_p),
            _time_first(new_caches[0]), _time_first(new_caches[1]), _time_first(new_caches[2]), jnp.stack(conv_s))
```

```python
import functools

import jax
import jax.numpy as jnp
from jax import lax
from jax.experimental import pallas as pl
from jax.experimental.pallas import tpu as pltpu

F32 = jnp.float32
BF16 = jnp.bfloat16

HEAD_DIM = 64
HEADS_PER_GROUP = 8
ATTN_GROUPS = ((128, 1), (512, 4), (2048, 16))
N_GROUPS = len(ATTN_GROUPS)
GROUP_W = HEADS_PER_GROUP * HEAD_DIM
QKV_W = N_GROUPS * GROUP_W
ATTN_BLOCK = 128
ATTN_SCALE = HEAD_DIM ** -0.5
CONV_WIDTH = 31
CONV_HIST = 32
N_KEYS = 128
PEER_HEADS = 8
PEER_TOPK = 16
NORM_EPS = 1e-6
NEG_INF = -1e30

VMEM_LIMIT_BYTES = 56 * 1024 * 1024
LANES = 128
HEAD_HALF = 4
ATTN_UNROLL = 4
HALF_W = HEAD_HALF * HEAD_DIM

NT_DIMS = (((1,), (1,)), ((), ()))


def _params(*sem, vmem_limit_bytes=VMEM_LIMIT_BYTES):
    return pltpu.CompilerParams(dimension_semantics=sem, vmem_limit_bytes=vmem_limit_bytes)


def _rmsnorm(x, g):
    return x * lax.rsqrt(jnp.mean(x * x, axis=-1, keepdims=True) + NORM_EPS) * g


def _sigmoid(x):
    return 1.0 / (1.0 + jnp.exp(-x))


def _in_proj_kernel(class_major, tm, seq, x_ref, g_ref, w_ref, *refs):
    if class_major:
        c_ref, ga_ref, gb_ref = refs[3:6]
        cm_refs = refs[6:15]
        tail_refs = refs[15:18]
        scr_ref = refs[18]
        tps = seq // tm
        last_tile = pl.program_id(0) % tps == tps - 1
    else:
        qkv_ref, c_ref, ga_ref, gb_ref = refs[:4]
    h = _rmsnorm(x_ref[...], g_ref[...]).astype(BF16)
    glu_a = None
    for j in range(w_ref.shape[1] // GROUP_W):
        res = jnp.dot(h, w_ref[:, j * GROUP_W:(j + 1) * GROUP_W], preferred_element_type=F32)
        if j < 9:
            kind, g = divmod(j, N_GROUPS)
            if class_major:
                if kind > 0:
                    keep = tail_refs[g].shape[-1]
                    if min(ATTN_GROUPS[g][0], seq) == seq:
                        tail_refs[g][0, 0, kind - 1] = res.T
                    else:
                        @pl.when(last_tile)
                        def _(res=res, g=g, kind=kind, keep=keep):
                            tail_refs[g][0, 0, kind - 1] = res[tm - keep:, :].T
                val = res * ATTN_SCALE if kind == 0 else res
                d = ATTN_GROUPS[g][1]
                cm = cm_refs[kind * N_GROUPS + g]
                if d == 1:
                    cm[0, 0] = val.astype(BF16)
                else:
                    for t in range(GROUP_W // LANES):
                        scr_ref[t] = val[:, t * LANES:(t + 1) * LANES]
                    for r in range(d):
                        cm[0, r] = jnp.concatenate(
                            [scr_ref[t, pl.ds(r, tm // d, stride=d), :] for t in range(GROUP_W // LANES)], axis=1).astype(BF16)
            else:
                qkv_ref[:, j * GROUP_W:(j + 1) * GROUP_W] = res
        elif j == 9:
            glu_a = res
        elif j == 10:
            c_ref[...] = glu_a * _sigmoid(res)
        elif j < 13:
            ga_ref[:, (j - 11) * GROUP_W:(j - 10) * GROUP_W] = _sigmoid(res)
        else:
            gb_ref[:, (j - 13) * GROUP_W:(j - 12) * GROUP_W] = _sigmoid(res)


def _in_proj(x, g, w_bf16, *, tm, class_major, batch=None, seq=None, layer=None, tails=None):
    t, d_model = x.shape
    in_w = w_bf16.shape[1]
    assert t % tm == 0 and in_w == 3 * QKV_W + 2 * GROUP_W + 2 * d_model and d_model == 2 * GROUP_W
    nt = t // tm
    row = lambda i: (i, 0)
    in_specs = [
        pl.BlockSpec((tm, d_model), row),
        pl.BlockSpec((1, d_model), lambda i: (0, 0)),
        pl.BlockSpec((d_model, in_w), lambda i: (0, 0), pipeline_mode=pl.Buffered(1)),
    ]
    common_shapes = [
        jax.ShapeDtypeStruct((t, GROUP_W), F32),
        jax.ShapeDtypeStruct((t, d_model), F32),
        jax.ShapeDtypeStruct((t, d_model), F32),
    ]
    common_specs = [pl.BlockSpec((tm, GROUP_W), row), pl.BlockSpec((tm, d_model), row), pl.BlockSpec((tm, d_model), row)]
    aliases = {}
    args = (x, g, w_bf16)
    if class_major:
        assert seq % tm == 0
        tps = seq // tm
        in_specs += [pl.BlockSpec(memory_space=pl.ANY)] * N_GROUPS
        args += tuple(tails)
        out_shape = list(common_shapes)
        out_specs = list(common_specs)
        for _kind in range(3):
            for (_, d) in ATTN_GROUPS:
                assert tm % (d * 16) == 0
                out_shape.append(jax.ShapeDtypeStruct((batch, d, seq // d, GROUP_W), BF16))
                out_specs.append(pl.BlockSpec((1, d, tm // d, GROUP_W), lambda i, tps=tps: (i // tps, 0, i % tps, 0)))
        for g_idx, ((window, _), tail) in enumerate(zip(ATTN_GROUPS, tails)):
            keep = min(window, seq)
            assert tail.shape[1:] == (batch, 2, GROUP_W, keep) and (keep == seq or (keep <= tm and keep % LANES == 0))
            aliases[3 + g_idx] = len(out_shape)
            out_shape.append(jax.ShapeDtypeStruct(tail.shape, F32))
            if keep == seq:
                out_specs.append(pl.BlockSpec((1, 1, 2, GROUP_W, tm), lambda i, tps=tps: (layer, i // tps, 0, 0, i % tps)))
            else:
                out_specs.append(pl.BlockSpec((1, 1, 2, GROUP_W, keep), lambda i, tps=tps: (layer, i // tps, 0, 0, 0)))
        scratch = [pltpu.VMEM((GROUP_W // LANES, tm, LANES), F32)]
    else:
        out_shape = [jax.ShapeDtypeStruct((t, 3 * QKV_W), F32)] + common_shapes
        out_specs = [pl.BlockSpec((tm, 3 * QKV_W), row)] + common_specs
        scratch = []
    return pl.pallas_call(
        functools.partial(_in_proj_kernel, class_major, tm, seq),
        grid=(nt,),
        in_specs=in_specs,
        out_specs=out_specs,
        out_shape=out_shape,
        scratch_shapes=scratch,
        input_output_aliases=aliases,
        compiler_params=_params("arbitrary"),
        name="in_proj_cm" if class_major else "in_proj_nat",
    )(*args)


def _attend_heads(qb, kb, vb, mask, n_heads):
    outs, lses = [], []
    for h in range(n_heads):
        sl = slice(h * HEAD_DIM, (h + 1) * HEAD_DIM)
        s = lax.dot_general(qb[:, sl], kb[:, sl], NT_DIMS, preferred_element_type=F32)
        s = jnp.where(mask, s, NEG_INF)
        m = jnp.max(s, axis=-1, keepdims=True)
        p = jnp.exp(s - m)
        den = jnp.sum(p, axis=-1, keepdims=True)
        o = jnp.dot(p.astype(BF16), vb[:, sl], preferred_element_type=F32) / den
        outs.append(o)
        lses.append(jnp.broadcast_to(m + jnp.log(den), o.shape))
    return jnp.concatenate(outs, axis=1), jnp.concatenate(lses, axis=1)


def _attn_prompt_kernel(seq, *refs):
    qkv_refs = refs[:9]
    out_ref = refs[9]
    ocm_ref, lcm_ref, m_ref, n_ref, d_ref = refs[10:]
    n_lt = HALF_W // LANES
    dist = (lax.broadcasted_iota(jnp.int32, (ATTN_BLOCK, 2 * ATTN_BLOCK), 0) + ATTN_BLOCK
            - lax.broadcasted_iota(jnp.int32, (ATTN_BLOCK, 2 * ATTN_BLOCK), 1))
    band_mask = (dist >= 0) & (dist <= ATTN_BLOCK)
    causal_mask = (lax.broadcasted_iota(jnp.int32, (ATTN_BLOCK, ATTN_BLOCK), 0)
                   >= lax.broadcasted_iota(jnp.int32, (ATTN_BLOCK, ATTN_BLOCK), 1))

    def put(ref, rows, val):
        for t in range(n_lt):
            ref[t, rows, :] = val[:, t * LANES:(t + 1) * LANES]

    for g, (window, d) in enumerate(ATTN_GROUPS):
        assert window // d == ATTN_BLOCK
        q_ref, k_ref, v_ref = qkv_refs[g], qkv_refs[N_GROUPS + g], qkv_refs[2 * N_GROUPS + g]
        sd = seq // d
        nb = sd // ATTN_BLOCK

        def block(r, i, q_ref=q_ref, k_ref=k_ref, v_ref=v_ref, sd=sd):
            if isinstance(i, int) and i == 0:
                q0 = 0
                o, l = _attend_heads(q_ref[0, r, 0:ATTN_BLOCK, :], k_ref[0, r, 0:ATTN_BLOCK, :],
                                     v_ref[0, r, 0:ATTN_BLOCK, :], causal_mask, HEAD_HALF)
            else:
                q0, k0 = i * ATTN_BLOCK, (i - 1) * ATTN_BLOCK
                if not isinstance(i, int):
                    q0, k0 = pl.multiple_of(q0, ATTN_BLOCK), pl.multiple_of(k0, ATTN_BLOCK)
                o, l = _attend_heads(q_ref[0, r, pl.ds(q0, ATTN_BLOCK), :], k_ref[0, r, pl.ds(k0, 2 * ATTN_BLOCK), :],
                                     v_ref[0, r, pl.ds(k0, 2 * ATTN_BLOCK), :], band_mask, HEAD_HALF)
            dst = r * sd + q0
            if not isinstance(dst, int):
                dst = pl.multiple_of(dst, ATTN_BLOCK)
            put(ocm_ref, pl.ds(dst, ATTN_BLOCK), o)
            put(lcm_ref, pl.ds(dst, ATTN_BLOCK), l)

        if nb == 1:
            assert d % ATTN_UNROLL == 0

            def classes(it, carry, block=block):
                for u in range(ATTN_UNROLL):
                    block(it * ATTN_UNROLL + u, 0)
                return carry

            lax.fori_loop(0, d // ATTN_UNROLL, classes, 0)
        else:
            assert nb % ATTN_UNROLL == 0

            def one_class(r, carry, block=block, nb=nb):
                for i in range(ATTN_UNROLL):
                    block(r, i)

                def later_blocks(it, c2):
                    for u in range(ATTN_UNROLL):
                        block(r, (it + 1) * ATTN_UNROLL + u)
                    return c2

                if nb > ATTN_UNROLL:
                    lax.fori_loop(0, nb // ATTN_UNROLL - 1, later_blocks, 0)
                return carry

            if d == 1:
                one_class(0, 0)
            else:
                lax.fori_loop(0, d, one_class, 0)

        if g == 0:
            assert d == 1
            m_ref[...] = lcm_ref[...]
            n_ref[...] = ocm_ref[...]
            d_ref[...] = jnp.ones(d_ref.shape, F32)
        else:
            for r in range(d):
                rows = pl.ds(r, sd, stride=d)
                for t in range(n_lt):
                    o_r = ocm_ref[t, r * sd:(r + 1) * sd, :]
                    l_r = lcm_ref[t, r * sd:(r + 1) * sd, :]
                    m_old = m_ref[t, rows, :]
                    m_new = jnp.maximum(m_old, l_r)
                    a = jnp.exp(m_old - m_new)
                    b = jnp.exp(l_r - m_new)
                    n_ref[t, rows, :] = n_ref[t, rows, :] * a + o_r * b
                    d_ref[t, rows, :] = d_ref[t, rows, :] * a + b
                    m_ref[t, rows, :] = m_new
    for t in range(n_lt):
        out_ref[0, :, t * LANES:(t + 1) * LANES] = (n_ref[t] / d_ref[t]).astype(BF16)


def _attn_prompt(cm, batch, seq):
    in_specs = []
    for _kind in range(3):
        for (_, d) in ATTN_GROUPS:
            assert seq % (d * ATTN_BLOCK) == 0
            in_specs.append(pl.BlockSpec((1, d, seq // d, HALF_W), lambda b, hh: (b, 0, 0, hh)))
    return pl.pallas_call(
        functools.partial(_attn_prompt_kernel, seq),
        grid=(batch, GROUP_W // HALF_W),
        in_specs=in_specs,
        out_specs=pl.BlockSpec((1, seq, HALF_W), lambda b, hh: (b, 0, hh)),
        out_shape=jax.ShapeDtypeStruct((batch, seq, GROUP_W), BF16),
        scratch_shapes=[pltpu.VMEM((HALF_W // LANES, seq, LANES), F32) for _ in range(5)],
        compiler_params=_params("parallel", "parallel"),
        name="attn_prompt",
    )(*cm)


def _attn_sample_kernel(ns, qkv_ref, c0_ref, c1_ref, c2_ref, _n0, _n1, _n2, out_ref, new0_ref, new1_ref, new2_ref):
    cache_refs = (c0_ref, c1_ref, c2_ref)
    new_refs = (new0_ref, new1_ref, new2_ref)
    qkv = qkv_ref[...]
    row_pad = jnp.zeros((LANES - ns, GROUP_W), F32)
    lane = lax.broadcasted_iota(jnp.int32, (HEADS_PER_GROUP, GROUP_W), 1)
    hrow = lax.broadcasted_iota(jnp.int32, (HEADS_PER_GROUP, GROUP_W), 0)
    head_mask = (lane // HEAD_DIM == hrow).astype(F32)
    nr = ns * HEADS_PER_GROUP
    n_new = 16
    assert ns <= n_new
    zpad = jnp.zeros((n_new - ns, GROUP_W), F32)
    outs, lses = [], []
    for g, (window, d) in enumerate(ATTN_GROUPS):
        assert window % d == 0 and d & (d - 1) == 0 and cache_refs[g].shape[2:] == (2, GROUP_W, window)
        k_t = cache_refs[g][0, 0, 0].astype(BF16)
        v_t = cache_refs[g][0, 0, 1].astype(BF16)
        q = qkv[:, g * GROUP_W:(g + 1) * GROUP_W] * ATTN_SCALE
        k_new32 = qkv[:, QKV_W + g * GROUP_W:QKV_W + (g + 1) * GROUP_W]
        v_new32 = qkv[:, 2 * QKV_W + g * GROUP_W:2 * QKV_W + (g + 1) * GROUP_W]
        for kv, new32 in enumerate((k_new32, v_new32)):
            new_t = jnp.concatenate([new32, row_pad], axis=0).T
            new_refs[g][0, 0, kv] = jnp.concatenate([cache_refs[g][0, 0, kv][:, ns:], new_t[:, :ns]], axis=1)
        k_new = jnp.concatenate([k_new32, zpad], axis=0).astype(BF16)
        v_new = jnp.concatenate([v_new32, zpad], axis=0).astype(BF16)
        q_exp = jnp.concatenate([q[n:n + 1] * head_mask for n in range(ns)], axis=0).astype(BF16)
        s_c = jnp.dot(q_exp, k_t, preferred_element_type=F32)
        s_n = lax.dot_general(q_exp, k_new, NT_DIMS, preferred_element_type=F32)
        delta_c = (lax.broadcasted_iota(jnp.int32, (nr, window), 1)
                   - lax.broadcasted_iota(jnp.int32, (nr, window), 0) // HEADS_PER_GROUP)
        delta_n = (lax.broadcasted_iota(jnp.int32, (nr, n_new), 0) // HEADS_PER_GROUP
                   - lax.broadcasted_iota(jnp.int32, (nr, n_new), 1))
        s_c = jnp.where((delta_c >= 0) & ((delta_c & (d - 1)) == 0), s_c, NEG_INF)
        s_n = jnp.where((delta_n >= 0) & ((delta_n & (d - 1)) == 0), s_n, NEG_INF)
        m = jnp.maximum(jnp.max(s_c, axis=-1, keepdims=True), jnp.max(s_n, axis=-1, keepdims=True))
        p_c = jnp.exp(s_c - m)
        p_n = jnp.exp(s_n - m)
        den = jnp.sum(p_c, axis=-1, keepdims=True) + jnp.sum(p_n, axis=-1, keepdims=True)
        pv = (lax.dot_general(p_c.astype(BF16), v_t, NT_DIMS, preferred_element_type=F32)
              + jnp.dot(p_n.astype(BF16), v_new, preferred_element_type=F32)) / den
        lse = jnp.broadcast_to(m + jnp.log(den), pv.shape)
        o_rows, l_rows = [], []
        for n in range(ns):
            blk = slice(n * HEADS_PER_GROUP, (n + 1) * HEADS_PER_GROUP)
            o_rows.append(jnp.sum(pv[blk] * head_mask, axis=0, keepdims=True))
            l_rows.append(jnp.sum(lse[blk] * head_mask, axis=0, keepdims=True))
        outs.append(jnp.concatenate(o_rows, axis=0))
        lses.append(jnp.concatenate(l_rows, axis=0))
    m = jnp.maximum(jnp.maximum(lses[0], lses[1]), lses[2])
    es = [jnp.exp(l - m) for l in lses]
    num = es[0] * outs[0] + es[1] * outs[1] + es[2] * outs[2]
    out_ref[...] = num / (es[0] + es[1] + es[2])


def _attn_sample(qkv, caches, new_caches, layer, dec_batch, ns):
    assert ns % 8 == 0
    in_specs = [pl.BlockSpec((ns, 3 * QKV_W), lambda b: (b, 0))]
    out_specs = [pl.BlockSpec((ns, GROUP_W), lambda b: (b, 0))]
    out_shape = [jax.ShapeDtypeStruct((dec_batch * ns, GROUP_W), F32)]
    for c in caches:
        spec = pl.BlockSpec((1, 1) + c.shape[2:], lambda b, layer=layer: (layer, b, 0, 0, 0))
        in_specs.append(spec)
        out_specs.append(spec)
        out_shape.append(jax.ShapeDtypeStruct(c.shape, F32))
    in_specs += [pl.BlockSpec(memory_space=pl.ANY)] * len(new_caches)
    n_in = 1 + len(caches)
    return pl.pallas_call(
        functools.partial(_attn_sample_kernel, ns),
        grid=(dec_batch,),
        in_specs=in_specs,
        out_specs=out_specs,
        out_shape=out_shape,
        input_output_aliases={n_in + g: 1 + g for g in range(len(new_caches))},
        compiler_params=_params("arbitrary"),
        name="attn_sample",
    )(qkv, *caches, *new_caches)


def _conv_kernel(tc, has_hist, *refs):
    if has_hist:
        c_ref, hist_ref, dw_ref, bias_ref, lng_ref, lnb_ref, out_ref, ext_ref = refs
        ext_ref[0:CONV_HIST, :] = hist_ref[0]
        ext_ref[CONV_HIST:, :] = c_ref[...]
    else:
        c_ref, dw_ref, bias_ref, lng_ref, lnb_ref, out_ref, ext_ref = refs
        i = pl.program_id(1)
        t0 = pl.multiple_of(i * tc, 8)
        prev = pl.multiple_of(jnp.maximum(t0 - CONV_HIST, 0), 8)
        hist = c_ref[pl.ds(prev, CONV_HIST), :]
        ext_ref[0:CONV_HIST, :] = jnp.where(i > 0, hist, 0.0)
        ext_ref[CONV_HIST:, :] = c_ref[pl.ds(t0, tc), :]
    off = CONV_HIST - (CONV_WIDTH - 1)
    acc = ext_ref[pl.ds(off, tc), :] * dw_ref[0:1, :]
    for k in range(1, CONV_WIDTH):
        acc = acc + ext_ref[pl.ds(off + k, tc), :] * dw_ref[k:k + 1, :]
    y = acc + bias_ref[...]
    mu = jnp.mean(y, axis=-1, keepdims=True)
    yc = y - mu
    var = jnp.mean(yc * yc, axis=-1, keepdims=True)
    y = yc * lax.rsqrt(var + NORM_EPS) * lng_ref[...] + lnb_ref[...]
    out_ref[...] = y * _sigmoid(y)


def _conv_prompt(c, dw, bias, lng, lnb, batch, seq, tc):
    assert seq % tc == 0 and tc % 8 == 0
    nt = seq // tc
    vec = lambda n: pl.BlockSpec((1, n), lambda b, i: (0, 0))
    return pl.pallas_call(
        functools.partial(_conv_kernel, tc, False),
        grid=(batch, nt),
        in_specs=[pl.BlockSpec((seq, GROUP_W), lambda b, i: (b, 0)),
                  pl.BlockSpec((CONV_WIDTH, GROUP_W), lambda b, i: (0, 0)), vec(GROUP_W), vec(GROUP_W), vec(GROUP_W)],
        out_specs=pl.BlockSpec((tc, GROUP_W), lambda b, i, nt=nt: (b * nt + i, 0)),
        out_shape=jax.ShapeDtypeStruct((batch * seq, GROUP_W), F32),
        scratch_shapes=[pltpu.VMEM((tc + CONV_HIST, GROUP_W), F32)],
        compiler_params=_params("parallel", "arbitrary"),
        name="conv_prompt",
    )(c, dw, bias, lng, lnb)


def _conv_sample(c, hist, dw, bias, lng, lnb, dec_batch, ns):
    vec = lambda n: pl.BlockSpec((1, n), lambda b: (0, 0))
    return pl.pallas_call(
        functools.partial(_conv_kernel, ns, True),
        grid=(dec_batch,),
        in_specs=[pl.BlockSpec((ns, GROUP_W), lambda b: (b, 0)),
                  pl.BlockSpec((1, CONV_HIST, GROUP_W), lambda b: (b, 0, 0)),
                  pl.BlockSpec((CONV_WIDTH, GROUP_W), lambda b: (0, 0)), vec(GROUP_W), vec(GROUP_W), vec(GROUP_W)],
        out_specs=pl.BlockSpec((ns, GROUP_W), lambda b: (b, 0)),
        out_shape=jax.ShapeDtypeStruct((dec_batch * ns, GROUP_W), F32),
        scratch_shapes=[pltpu.VMEM((ns + CONV_HIST, GROUP_W), F32)],
        compiler_params=_params("parallel"),
        name="conv_sample",
    )(c, hist, dw, bias, lng, lnb)


def _merge_kernel(x_ref, attn_ref, u_ref, ga_ref, gb_ref, wa_ref, wc_ref, bc_ref, wo_ref, out_ref):
    y_a = jnp.dot(attn_ref[...].astype(BF16), wa_ref[...], preferred_element_type=F32)
    y_b = jnp.dot(u_ref[...].astype(BF16), wc_ref[...], preferred_element_type=F32) + bc_ref[...]
    y = (ga_ref[...] * y_a + gb_ref[...] * y_b).astype(BF16)
    out_ref[...] = x_ref[...] + jnp.dot(y, wo_ref[...], preferred_element_type=F32)


def _merge(x, attn, u, ga, gb, wa, wc, bc, wo, *, tm):
    t, d_model = x.shape
    assert t % tm == 0
    row = lambda i: (i, 0)
    const = lambda i: (0, 0)
    return pl.pallas_call(
        _merge_kernel,
        grid=(t // tm,),
        in_specs=[pl.BlockSpec((tm, d_model), row), pl.BlockSpec((tm, GROUP_W), row), pl.BlockSpec((tm, GROUP_W), row),
                  pl.BlockSpec((tm, d_model), row), pl.BlockSpec((tm, d_model), row),
                  pl.BlockSpec((GROUP_W, d_model), const), pl.BlockSpec((GROUP_W, d_model), const),
                  pl.BlockSpec((1, d_model), const), pl.BlockSpec((d_model, d_model), const)],
        out_specs=pl.BlockSpec((tm, d_model), row),
        out_shape=jax.ShapeDtypeStruct((t, d_model), F32),
        compiler_params=_params("parallel"),
        name="merge",
    )(x, attn, u, ga, gb, wa, wc, bc, wo)


_NO_RANK = 64.0


def _top_values(s, out_ref, n, want_rank=False):
    cur = s
    rank = jnp.full(s.shape, _NO_RANK, F32) if want_rank else None
    for j in range(n):
        m = jnp.max(cur, axis=0, keepdims=True)
        out_ref[j:j + 1, :] = m
        hit = cur == m
        if want_rank:
            rank = jnp.where(hit, float(j), rank)
        if j + 1 < n:
            cur = jnp.where(hit, -jnp.inf, cur)
    return rank


def _split_bf16(x):
    hi = x.astype(BF16)
    return hi, (x - hi.astype(F32)).astype(BF16)


def _dot_bf16x3(a_hi, a_lo, b_hi, b_lo):
    d = functools.partial(jnp.dot, preferred_element_type=F32)
    return d(a_hi, b_hi) + (d(a_hi, b_lo) + d(a_lo, b_hi))


_CAND = [(i, PEER_TOPK // (i + 1)) for i in range(PEER_TOPK)]
_CAND_ROWS = 64
PEER_SLAB = 256
GATE_ROWS = 16


def _peer_kernel(tp, ech, lb, n_chunks, x_ref, g_ref, wqh_ref, wql_ref, kh_ref, kl_ref, u_ref, vt_ref, out_ref,
                 ht_ref, htl_ref, meta_ref, sa_ref, rank_ref, e2_ref, at_ref, gt_ref, acc_ref,
                 a_ref, b_ref, cand_ref, top_ref):
    c = pl.program_id(1)
    n_lb = tp // lb

    @pl.when(c == 0)
    def _route():
        for t0 in range(0, tp, TOKEN_BLOCK):
            tok = slice(t0, t0 + TOKEN_BLOCK)
            ht = _rmsnorm(x_ref[tok, :], g_ref[...]).T
            ht_hi, ht_lo = _split_bf16(ht)
            ht_ref[:, tok] = ht_hi
            htl_ref[:, tok] = ht_lo
        acc_ref[...] = jnp.zeros(acc_ref.shape, F32)
        gt_ref[1] = jnp.zeros(gt_ref.shape[1:], BF16)

        def scores(h, s_ref):
            for half in range(2):
                k = 2 * h + half
                rows = pl.ds(pl.multiple_of(k * N_KEYS, N_KEYS), N_KEYS)
                q = _dot_bf16x3(wqh_ref[rows, :], wql_ref[rows, :], ht_ref[...], htl_ref[...])
                q_hi, q_lo = _split_bf16(q)
                s_ref[half] = _dot_bf16x3(kh_ref[k], kl_ref[k], q_hi, q_lo)

        def select(h, s_ref):
            for j in range(n_lb):
                cols = slice(j * lb, (j + 1) * lb)
                s1 = s_ref[0, :, cols]
                s2 = s_ref[1, :, cols]
                _top_values(s1, a_ref, PEER_TOPK)
                rank2 = _top_values(s2, b_ref, PEER_TOPK, want_rank=True)
                cand_ref[...] = jnp.full(cand_ref.shape, -jnp.inf, F32)
                row = 0
                for (i, n) in _CAND:
                    cand_ref[row:row + n, :] = a_ref[i:i + 1, :] + b_ref[0:n, :]
                    row += n
                _top_values(cand_ref[...], top_ref, PEER_TOPK)
                top = top_ref[...]
                tau = top[PEER_TOPK - 1:PEER_TOPK, :]
                z = jnp.sum(jnp.exp(top - top[0:1, :]), axis=0, keepdims=True)
                a0, a15 = a_ref[0:1, :], a_ref[PEER_TOPK - 1:PEER_TOPK, :]
                b0 = b_ref[0:1, :]
                cnt = jnp.zeros(s1.shape, F32)
                for r in range(PEER_TOPK):
                    cnt = cnt + jnp.where(s1 + b_ref[r:r + 1, :] >= tau, 1.0, 0.0)
                e1_bits = lax.bitcast_convert_type(jnp.exp(s1 - a0).astype(BF16).astype(F32), jnp.uint32)
                cnt_bits = jnp.where(s1 >= a15, cnt, 0.0).astype(jnp.int32).astype(jnp.uint32)
                meta_ref[h, :, cols] = e1_bits | cnt_bits
                rank_ref[h, :, cols] = rank2.astype(BF16)
                e2_ref[h, :, cols] = (jnp.exp(s2 - b0) / z).astype(BF16)

        def one_head(h, carry):
            scores(h, sa_ref)
            select(h, sa_ref)
            return carry

        lax.fori_loop(0, PEER_HEADS, one_head, 0)

    n_slabs = ech // PEER_SLAB
    d_part = acc_ref.shape[0] // n_slabs

    cur = c % 2
    prev = 1 - cur

    def output_part(p):
        rows = slice(p * d_part, (p + 1) * d_part)
        acc_ref[rows, :] += jnp.dot(vt_ref[0, rows, :], gt_ref[prev], preferred_element_type=F32)

    def pre_activations(si):
        slab = slice(si * PEER_SLAB, (si + 1) * PEER_SLAB)
        at_ref[si % 2] = jnp.dot(u_ref[slab, :], ht_ref[...], preferred_element_type=F32)

    keys_per_chunk = ech // N_KEYS
    chunks_per_tile = META_TILE // keys_per_chunk

    def gate_slab(si, sub):
        i1_tile = pl.ds(pl.multiple_of((c // chunks_per_tile) * META_TILE, META_TILE), META_TILE)
        s0 = si * PEER_SLAB
        for i1l in range(s0 // N_KEYS, (s0 + PEER_SLAB) // N_KEYS):
            rows = slice(i1l * N_KEYS, (i1l + 1) * N_KEYS)
            r = sub * keys_per_chunk + i1l
            for j in range(n_lb):
                cols = slice(j * lb, (j + 1) * lb)
                w = jnp.zeros((N_KEYS, lb), BF16)
                for h in range(PEER_HEADS):
                    word = meta_ref[h, i1_tile, cols][r:r + 1]
                    cnt_row = (word & jnp.uint32(0xFFFF)).astype(jnp.int32).astype(F32)
                    e1_row = lax.bitcast_convert_type(word & jnp.uint32(0xFFFF0000), F32)
                    cnt_b = jnp.broadcast_to(cnt_row, (N_KEYS, lb)).astype(BF16)
                    e1_b = jnp.broadcast_to(e1_row, (N_KEYS, lb)).astype(BF16)
                    keep = rank_ref[h, :, cols] < cnt_b
                    w = w + jnp.where(keep, e2_ref[h, :, cols] * e1_b, jnp.zeros((), BF16))
                a = at_ref[si % 2, (i1l * N_KEYS - s0):(i1l * N_KEYS - s0) + N_KEYS, cols]
                act = 0.5 * a * (1.0 + lax.erf(a * (0.5 ** 0.5)))
                gt_ref[cur, rows, cols] = w * act.astype(BF16)

    for sub in range(chunks_per_tile):
        @pl.when((c < n_chunks) & (c % chunks_per_tile == sub))
        def _chunk(sub=sub):
            pre_activations(0)
            for si in range(n_slabs):
                if si + 1 < n_slabs:
                    pre_activations(si + 1)
                output_part(si)
                gate_slab(si, sub)

    @pl.when(c == n_chunks)
    def _drain():
        for p in range(n_slabs):
            output_part(p)
        for t0 in range(0, tp, TOKEN_BLOCK):
            tok = slice(t0, t0 + TOKEN_BLOCK)
            out_ref[tok, :] = x_ref[tok, :] + acc_ref[:, tok].T


PEER_CHUNK = 4 * N_KEYS
META_TILE = 8
PEER_VMEM_LIMIT_BYTES = 60000 * 1024
TOKEN_BLOCK = 128
PEER_TOKENS = 1024


def _peer(x, g, wq_t, keys, u_bf16, vt_bf16, *, tp, lb=LANES):
    t, d_model = x.shape
    n_exp = u_bf16.shape[0]
    ech = PEER_CHUNK
    n_chunks = n_exp // ech
    assert t % tp == 0 and n_exp % ech == 0 and tp % lb == 0 and n_exp == N_KEYS * N_KEYS
    assert vt_bf16.shape == (n_chunks, d_model, ech)
    q_w = wq_t[0].shape[0]
    assert q_w == PEER_HEADS * 2 * N_KEYS and ech % PEER_SLAB == 0 and d_model % (ech // PEER_SLAB) == 0
    meta = lambda dt: pltpu.VMEM((PEER_HEADS, N_KEYS, tp), dt)
    wq_spec = pl.BlockSpec((q_w, d_model), lambda i, c: (0, 0), pipeline_mode=pl.Buffered(1))
    keys_spec = pl.BlockSpec((2 * PEER_HEADS, N_KEYS, N_KEYS), lambda i, c: (0, 0, 0), pipeline_mode=pl.Buffered(1))
    return pl.pallas_call(
        functools.partial(_peer_kernel, tp, ech, lb, n_chunks),
        grid=(t // tp, n_chunks + 1),
        in_specs=[pl.BlockSpec((tp, d_model), lambda i, c: (i, 0), pipeline_mode=pl.Buffered(1)),
                  pl.BlockSpec((1, d_model), lambda i, c: (0, 0)),
                  wq_spec, wq_spec, keys_spec, keys_spec,
                  pl.BlockSpec((ech, d_model), lambda i, c: (jnp.minimum(c, n_chunks - 1), 0)),
                  pl.BlockSpec((1, d_model, ech), lambda i, c: (jnp.maximum(c - 1, 0), 0, 0))],
        out_specs=pl.BlockSpec((tp, d_model), lambda i, c: (i, 0), pipeline_mode=pl.Buffered(1)),
        out_shape=jax.ShapeDtypeStruct((t, d_model), F32),
        scratch_shapes=[pltpu.VMEM((d_model, tp), BF16),
                        pltpu.VMEM((d_model, tp), BF16),
                        meta(jnp.uint32),
                        pltpu.VMEM((2, N_KEYS, tp), F32),
                        meta(BF16), meta(BF16),
                        pltpu.VMEM((2, PEER_SLAB, tp), F32),
                        pltpu.VMEM((2, ech, tp), BF16),
                        pltpu.VMEM((d_model, tp), F32),
                        pltpu.VMEM((PEER_TOPK, lb), F32), pltpu.VMEM((PEER_TOPK, lb), F32),
                        pltpu.VMEM((_CAND_ROWS, lb), F32), pltpu.VMEM((PEER_TOPK, lb), F32)],
        compiler_params=_params("parallel", "arbitrary", vmem_limit_bytes=PEER_VMEM_LIMIT_BYTES),
        name="peer",
    )(x, g, wq_t[0], wq_t[1], keys[0], keys[1], u_bf16, vt_bf16)


def _final_norm_kernel(x_ref, g_ref, out_ref):
    out_ref[...] = _rmsnorm(x_ref[...], g_ref[...])


def _final_norm(x, g, *, tm):
    t, d_model = x.shape
    return pl.pallas_call(
        _final_norm_kernel,
        grid=(t // tm,),
        in_specs=[pl.BlockSpec((tm, d_model), lambda i: (i, 0)), pl.BlockSpec((1, d_model), lambda i: (0, 0))],
        out_specs=pl.BlockSpec((tm, d_model), lambda i: (i, 0)),
        out_shape=jax.ShapeDtypeStruct((t, d_model), F32),
        compiler_params=_params("parallel"),
        name="final_norm",
    )(x, g)


def _time_last(buf):
    return buf.transpose(0, 1, 3, 4, 5, 2).reshape(buf.shape[0], buf.shape[1], 2, GROUP_W, buf.shape[2])


def _time_first(buf):
    depth, batch, _, _, time = buf.shape
    return buf.reshape(depth, batch, 2, HEADS_PER_GROUP, HEAD_DIM, time).transpose(0, 1, 5, 2, 3, 4)


def kernel(x_prompt, x_sample, cache_kv_w128, cache_kv_w512, cache_kv_w2048, state_conv, norm_mix_g, norm_ffn_g, w_in, dw_kernel, dw_bias, conv_ln_g, conv_ln_b, w_conv_out, b_conv_out, w_attn_out, w_out, w_peer_q, peer_sub_keys, peer_u, peer_v, norm_final_g):
    batch, seq, d_model = x_prompt.shape
    dec_batch, ns, _ = x_sample.shape
    depth = w_in.shape[0]
    tp_tokens, ts_tokens = batch * seq, dec_batch * ns
    tm_p = 512
    cache_in = (cache_kv_w128, cache_kv_w512, cache_kv_w2048)
    for c, (window, _) in zip(cache_in, ATTN_GROUPS):
        assert c.shape[2] == window
    caches = [_time_last(c) for c in cache_in]
    assert state_conv.shape[2] == CONV_WIDTH - 1
    hist_pad = CONV_HIST - (CONV_WIDTH - 1)

    xp = x_prompt.reshape(tp_tokens, d_model)
    xs = x_sample.reshape(ts_tokens, d_model)
    tails = [jnp.zeros((depth, batch, 2, GROUP_W, min(window, seq)), F32) for (window, _) in ATTN_GROUPS]
    new_caches = [jnp.zeros(c.shape, F32) for c in caches]
    conv_p, conv_s = [], []
    vec = lambda a: a.reshape(1, -1)
    for l in range(depth):
        w_in_b = w_in[l].astype(BF16)
        wa_b, wc_b, wo_b = w_attn_out[l].astype(BF16), w_conv_out[l].astype(BF16), w_out[l].astype(BF16)
        wq_t = _split_bf16(w_peer_q[l].T)
        keys = _split_bf16(peer_sub_keys[l].reshape(2 * PEER_HEADS, N_KEYS, -1))
        u_b = peer_u[l].astype(BF16)
        vt_b = peer_v[l].reshape(-1, PEER_CHUNK, d_model).transpose(0, 2, 1).astype(BF16)
        conv_w = (dw_kernel[l], vec(dw_bias[l]), vec(conv_ln_g[l]), vec(conv_ln_b[l]))
        mix_g, ffn_g = vec(norm_mix_g[l]), vec(norm_ffn_g[l])

        outs = _in_proj(xp, mix_g, w_in_b, tm=tm_p, class_major=True, batch=batch, seq=seq, layer=l, tails=tails)
        c, ga, gb = outs[:3]
        tails = list(outs[12:])
        attn = _attn_prompt(outs[3:12], batch, seq).reshape(tp_tokens, GROUP_W)
        u = _conv_prompt(c, *conv_w, batch, seq, 512)
        xp = _merge(xp, attn, u, ga, gb, wa_b, wc_b, vec(b_conv_out[l]), wo_b, tm=tm_p)
        xp = _peer(xp, ffn_g, wq_t, keys, u_b, vt_b, tp=PEER_TOKENS)
        conv_p.append(c.reshape(batch, seq, GROUP_W)[:, seq - (CONV_WIDTH - 1):])

        qkv, c, ga, gb = _in_proj(xs, mix_g, w_in_b, tm=ts_tokens, class_major=False)
        attn, *new_caches = _attn_sample(qkv, caches, new_caches, l, dec_batch, ns)
        hist = jnp.pad(state_conv[l], ((0, 0), (hist_pad, 0), (0, 0)))
        u = _conv_sample(c, hist, *conv_w, dec_batch, ns)
        xs = _merge(xs, attn, u, ga, gb, wa_b, wc_b, vec(b_conv_out[l]), wo_b, tm=ts_tokens)
        xs = _peer(xs, ffn_g, wq_t, keys, u_b, vt_b, tp=ts_tokens)
        full = jnp.concatenate([state_conv[l], c.reshape(dec_batch, ns, GROUP_W)], axis=1)
        conv_s.append(full[:, full.shape[1] - (CONV_WIDTH - 1):])

    y_prompt = _final_norm(xp, vec(norm_final_g), tm=tm_p).reshape(batch, seq, d_model)
    y_sample = _final_norm(xs, vec(norm_final_g), tm=ts_tokens).reshape(dec_batch, ns, d_model)
    return (y_prompt, y_sample,
            _time_first(tails[0]), _time_first(tails[1]), _time_first(tails[2]), jnp.stack(conv_p),
            _time_first(new_caches[0]), _time_first(new_caches[1]), _time_first(new_caches[2]), jnp.stack(conv_s))
```

```python
import functools

import jax
import jax.numpy as jnp
from jax import lax
from jax.experimental import pallas as pl
from jax.experimental.pallas import tpu as pltpu

F32 = jnp.float32
BF16 = jnp.bfloat16

HEAD_DIM = 64
HEADS_PER_GROUP = 8
ATTN_GROUPS = ((128, 1), (512, 4), (2048, 16))
N_GROUPS = len(ATTN_GROUPS)
GROUP_W = HEADS_PER_GROUP * HEAD_DIM
QKV_W = N_GROUPS * GROUP_W
ATTN_BLOCK = 128
ATTN_SCALE = HEAD_DIM ** -0.5
CONV_WIDTH = 31
CONV_HIST = 32
N_KEYS = 128
PEER_HEADS = 8
PEER_TOPK = 16
NORM_EPS = 1e-6
NEG_INF = -1e30

VMEM_LIMIT_BYTES = 56 * 1024 * 1024
LANES = 128
HEAD_HALF = 4
ATTN_UNROLL = 4
HALF_W = HEAD_HALF * HEAD_DIM

NT_DIMS = (((1,), (1,)), ((), ()))


def _params(*sem, flags=None):
    return pltpu.CompilerParams(dimension_semantics=sem, vmem_limit_bytes=VMEM_LIMIT_BYTES, flags=flags)


def _rmsnorm(x, g):
    return x * lax.rsqrt(jnp.mean(x * x, axis=-1, keepdims=True) + NORM_EPS) * g


def _sigmoid(x):
    return 1.0 / (1.0 + jnp.exp(-x))


def _in_proj_kernel(class_major, tm, seq, x_ref, g_ref, w_ref, *refs):
    if class_major:
        c_ref, ga_ref, gb_ref = refs[3:6]
        cm_refs = refs[6:15]
        tail_refs = refs[15:18]
        scr_ref = refs[18]
        tps = seq // tm
        last_tile = pl.program_id(0) % tps == tps - 1
    else:
        qkv_ref, c_ref, ga_ref, gb_ref = refs[:4]
    h = _rmsnorm(x_ref[...], g_ref[...]).astype(BF16)
    glu_a = None
    for j in range(w_ref.shape[1] // GROUP_W):
        res = jnp.dot(h, w_ref[:, j * GROUP_W:(j + 1) * GROUP_W], preferred_element_type=F32)
        if j < 9:
            kind, g = divmod(j, N_GROUPS)
            if class_major:
                if kind > 0:
                    keep = tail_refs[g].shape[-1]
                    if min(ATTN_GROUPS[g][0], seq) == seq:
                        tail_refs[g][0, 0, kind - 1] = res.T
                    else:
                        @pl.when(last_tile)
                        def _(res=res, g=g, kind=kind, keep=keep):
                            tail_refs[g][0, 0, kind - 1] = res[tm - keep:, :].T
                val = res * ATTN_SCALE if kind == 0 else res
                d = ATTN_GROUPS[g][1]
                cm = cm_refs[kind * N_GROUPS + g]
                if d == 1:
                    cm[0, 0] = val.astype(BF16)
                else:
                    for t in range(GROUP_W // LANES):
                        scr_ref[t] = val[:, t * LANES:(t + 1) * LANES]
                    for r in range(d):
                        cm[0, r] = jnp.concatenate(
                            [scr_ref[t, pl.ds(r, tm // d, stride=d), :] for t in range(GROUP_W // LANES)], axis=1).astype(BF16)
            else:
                qkv_ref[:, j * GROUP_W:(j + 1) * GROUP_W] = res
        elif j == 9:
            glu_a = res
        elif j == 10:
            c_ref[...] = glu_a * _sigmoid(res)
        elif j < 13:
            ga_ref[:, (j - 11) * GROUP_W:(j - 10) * GROUP_W] = _sigmoid(res)
        else:
            gb_ref[:, (j - 13) * GROUP_W:(j - 12) * GROUP_W] = _sigmoid(res)


def _in_proj(x, g, w_bf16, *, tm, class_major, batch=None, seq=None, layer=None, tails=None):
    t, d_model = x.shape
    in_w = w_bf16.shape[1]
    assert t % tm == 0 and in_w == 3 * QKV_W + 2 * GROUP_W + 2 * d_model and d_model == 2 * GROUP_W
    nt = t // tm
    row = lambda i: (i, 0)
    in_specs = [
        pl.BlockSpec((tm, d_model), row),
        pl.BlockSpec((1, d_model), lambda i: (0, 0)),
        pl.BlockSpec((d_model, in_w), lambda i: (0, 0), pipeline_mode=pl.Buffered(1)),
    ]
    common_shapes = [
        jax.ShapeDtypeStruct((t, GROUP_W), F32),
        jax.ShapeDtypeStruct((t, d_model), F32),
        jax.ShapeDtypeStruct((t, d_model), F32),
    ]
    common_specs = [pl.BlockSpec((tm, GROUP_W), row), pl.BlockSpec((tm, d_model), row), pl.BlockSpec((tm, d_model), row)]
    aliases = {}
    args = (x, g, w_bf16)
    if class_major:
        assert seq % tm == 0
        tps = seq // tm
        in_specs += [pl.BlockSpec(memory_space=pl.ANY)] * N_GROUPS
        args += tuple(tails)
        out_shape = list(common_shapes)
        out_specs = list(common_specs)
        for _kind in range(3):
            for (_, d) in ATTN_GROUPS:
                assert tm % (d * 16) == 0
                out_shape.append(jax.ShapeDtypeStruct((batch, d, seq // d, GROUP_W), BF16))
                out_specs.append(pl.BlockSpec((1, d, tm // d, GROUP_W), lambda i, tps=tps: (i // tps, 0, i % tps, 0)))
        for g_idx, ((window, _), tail) in enumerate(zip(ATTN_GROUPS, tails)):
            keep = min(window, seq)
            assert tail.shape[1:] == (batch, 2, GROUP_W, keep) and (keep == seq or (keep <= tm and keep % LANES == 0))
            aliases[3 + g_idx] = len(out_shape)
            out_shape.append(jax.ShapeDtypeStruct(tail.shape, F32))
            if keep == seq:
                out_specs.append(pl.BlockSpec((1, 1, 2, GROUP_W, tm), lambda i, tps=tps: (layer, i // tps, 0, 0, i % tps)))
            else:
                out_specs.append(pl.BlockSpec((1, 1, 2, GROUP_W, keep), lambda i, tps=tps: (layer, i // tps, 0, 0, 0)))
        scratch = [pltpu.VMEM((GROUP_W // LANES, tm, LANES), F32)]
    else:
        out_shape = [jax.ShapeDtypeStruct((t, 3 * QKV_W), F32)] + common_shapes
        out_specs = [pl.BlockSpec((tm, 3 * QKV_W), row)] + common_specs
        scratch = []
    return pl.pallas_call(
        functools.partial(_in_proj_kernel, class_major, tm, seq),
        grid=(nt,),
        in_specs=in_specs,
        out_specs=out_specs,
        out_shape=out_shape,
        scratch_shapes=scratch,
        input_output_aliases=aliases,
        compiler_params=_params("arbitrary"),
        name="in_proj_cm" if class_major else "in_proj_nat",
    )(*args)


def _attend_heads(qb, kb, vb, mask, n_heads):
    outs, lses = [], []
    for h in range(n_heads):
        sl = slice(h * HEAD_DIM, (h + 1) * HEAD_DIM)
        s = lax.dot_general(qb[:, sl], kb[:, sl], NT_DIMS, preferred_element_type=F32)
        s = jnp.where(mask, s, NEG_INF)
        m = jnp.max(s, axis=-1, keepdims=True)
        p = jnp.exp(s - m)
        den = jnp.sum(p, axis=-1, keepdims=True)
        o = jnp.dot(p.astype(BF16), vb[:, sl], preferred_element_type=F32) / den
        outs.append(o)
        lses.append(jnp.broadcast_to(m + jnp.log(den), o.shape))
    return jnp.concatenate(outs, axis=1), jnp.concatenate(lses, axis=1)


def _attn_prompt_kernel(seq, *refs):
    qkv_refs = refs[:9]
    out_ref = refs[9]
    ocm_ref, lcm_ref, m_ref, n_ref, d_ref = refs[10:]
    n_lt = HALF_W // LANES
    dist = (lax.broadcasted_iota(jnp.int32, (ATTN_BLOCK, 2 * ATTN_BLOCK), 0) + ATTN_BLOCK
            - lax.broadcasted_iota(jnp.int32, (ATTN_BLOCK, 2 * ATTN_BLOCK), 1))
    band_mask = (dist >= 0) & (dist <= ATTN_BLOCK)
    causal_mask = (lax.broadcasted_iota(jnp.int32, (ATTN_BLOCK, ATTN_BLOCK), 0)
                   >= lax.broadcasted_iota(jnp.int32, (ATTN_BLOCK, ATTN_BLOCK), 1))

    def put(ref, rows, val):
        for t in range(n_lt):
            ref[t, rows, :] = val[:, t * LANES:(t + 1) * LANES]

    for g, (window, d) in enumerate(ATTN_GROUPS):
        assert window // d == ATTN_BLOCK
        q_ref, k_ref, v_ref = qkv_refs[g], qkv_refs[N_GROUPS + g], qkv_refs[2 * N_GROUPS + g]
        sd = seq // d
        nb = sd // ATTN_BLOCK

        def block(r, i, q_ref=q_ref, k_ref=k_ref, v_ref=v_ref, sd=sd):
            if isinstance(i, int) and i == 0:
                q0 = 0
                o, l = _attend_heads(q_ref[0, r, 0:ATTN_BLOCK, :], k_ref[0, r, 0:ATTN_BLOCK, :],
                                     v_ref[0, r, 0:ATTN_BLOCK, :], causal_mask, HEAD_HALF)
            else:
                q0, k0 = i * ATTN_BLOCK, (i - 1) * ATTN_BLOCK
                if not isinstance(i, int):
                    q0, k0 = pl.multiple_of(q0, ATTN_BLOCK), pl.multiple_of(k0, ATTN_BLOCK)
                o, l = _attend_heads(q_ref[0, r, pl.ds(q0, ATTN_BLOCK), :], k_ref[0, r, pl.ds(k0, 2 * ATTN_BLOCK), :],
                                     v_ref[0, r, pl.ds(k0, 2 * ATTN_BLOCK), :], band_mask, HEAD_HALF)
            dst = r * sd + q0
            if not isinstance(dst, int):
                dst = pl.multiple_of(dst, ATTN_BLOCK)
            put(ocm_ref, pl.ds(dst, ATTN_BLOCK), o)
            put(lcm_ref, pl.ds(dst, ATTN_BLOCK), l)

        if nb == 1:
            assert d % ATTN_UNROLL == 0

            def classes(it, carry, block=block):
                for u in range(ATTN_UNROLL):
                    block(it * ATTN_UNROLL + u, 0)
                return carry

            lax.fori_loop(0, d // ATTN_UNROLL, classes, 0)
        else:
            assert nb % ATTN_UNROLL == 0

            def one_class(r, carry, block=block, nb=nb):
                for i in range(ATTN_UNROLL):
                    block(r, i)

                def later_blocks(it, c2):
                    for u in range(ATTN_UNROLL):
                        block(r, (it + 1) * ATTN_UNROLL + u)
                    return c2

                if nb > ATTN_UNROLL:
                    lax.fori_loop(0, nb // ATTN_UNROLL - 1, later_blocks, 0)
                return carry

            if d == 1:
                one_class(0, 0)
            else:
                lax.fori_loop(0, d, one_class, 0)

        if g == 0:
            assert d == 1
            m_ref[...] = lcm_ref[...]
            n_ref[...] = ocm_ref[...]
            d_ref[...] = jnp.ones(d_ref.shape, F32)
        else:
            for r in range(d):
                rows = pl.ds(r, sd, stride=d)
                for t in range(n_lt):
                    o_r = ocm_ref[t, r * sd:(r + 1) * sd, :]
                    l_r = lcm_ref[t, r * sd:(r + 1) * sd, :]
                    m_old = m_ref[t, rows, :]
                    m_new = jnp.maximum(m_old, l_r)
                    a = jnp.exp(m_old - m_new)
                    b = jnp.exp(l_r - m_new)
                    n_ref[t, rows, :] = n_ref[t, rows, :] * a + o_r * b
                    d_ref[t, rows, :] = d_ref[t, rows, :] * a + b
                    m_ref[t, rows, :] = m_new
    for t in range(n_lt):
        out_ref[0, :, t * LANES:(t + 1) * LANES] = (n_ref[t] / d_ref[t]).astype(BF16)


def _attn_prompt(cm, batch, seq):
    in_specs = []
    for _kind in range(3):
        for (_, d) in ATTN_GROUPS:
            assert seq % (d * ATTN_BLOCK) == 0
            in_specs.append(pl.BlockSpec((1, d, seq // d, HALF_W), lambda b, hh: (b, 0, 0, hh)))
    return pl.pallas_call(
        functools.partial(_attn_prompt_kernel, seq),
        grid=(batch, GROUP_W // HALF_W),
        in_specs=in_specs,
        out_specs=pl.BlockSpec((1, seq, HALF_W), lambda b, hh: (b, 0, hh)),
        out_shape=jax.ShapeDtypeStruct((batch, seq, GROUP_W), BF16),
        scratch_shapes=[pltpu.VMEM((HALF_W // LANES, seq, LANES), F32) for _ in range(5)],
        compiler_params=_params("parallel", "parallel"),
        name="attn_prompt",
    )(*cm)


def _attn_sample_kernel(ns, qkv_ref, c0_ref, c1_ref, c2_ref, _n0, _n1, _n2, out_ref, new0_ref, new1_ref, new2_ref):
    cache_refs = (c0_ref, c1_ref, c2_ref)
    new_refs = (new0_ref, new1_ref, new2_ref)
    qkv = qkv_ref[...]
    row_pad = jnp.zeros((LANES - ns, GROUP_W), F32)
    lane = lax.broadcasted_iota(jnp.int32, (HEADS_PER_GROUP, GROUP_W), 1)
    hrow = lax.broadcasted_iota(jnp.int32, (HEADS_PER_GROUP, GROUP_W), 0)
    head_mask = (lane // HEAD_DIM == hrow).astype(F32)
    nr = ns * HEADS_PER_GROUP
    n_new = 16
    assert ns <= n_new
    zpad = jnp.zeros((n_new - ns, GROUP_W), F32)
    outs, lses = [], []
    for g, (window, d) in enumerate(ATTN_GROUPS):
        assert window % d == 0 and d & (d - 1) == 0 and cache_refs[g].shape[2:] == (2, GROUP_W, window)
        k_t = cache_refs[g][0, 0, 0].astype(BF16)
        v_t = cache_refs[g][0, 0, 1].astype(BF16)
        q = qkv[:, g * GROUP_W:(g + 1) * GROUP_W] * ATTN_SCALE
        k_new32 = qkv[:, QKV_W + g * GROUP_W:QKV_W + (g + 1) * GROUP_W]
        v_new32 = qkv[:, 2 * QKV_W + g * GROUP_W:2 * QKV_W + (g + 1) * GROUP_W]
        for kv, new32 in enumerate((k_new32, v_new32)):
            new_t = jnp.concatenate([new32, row_pad], axis=0).T
            new_refs[g][0, 0, kv] = jnp.concatenate([cache_refs[g][0, 0, kv][:, ns:], new_t[:, :ns]], axis=1)
        k_new = jnp.concatenate([k_new32, zpad], axis=0).astype(BF16)
        v_new = jnp.concatenate([v_new32, zpad], axis=0).astype(BF16)
        q_exp = jnp.concatenate([q[n:n + 1] * head_mask for n in range(ns)], axis=0).astype(BF16)
        s_c = jnp.dot(q_exp, k_t, preferred_element_type=F32)
        s_n = lax.dot_general(q_exp, k_new, NT_DIMS, preferred_element_type=F32)
        delta_c = (lax.broadcasted_iota(jnp.int32, (nr, window), 1)
                   - lax.broadcasted_iota(jnp.int32, (nr, window), 0) // HEADS_PER_GROUP)
        delta_n = (lax.broadcasted_iota(jnp.int32, (nr, n_new), 0) // HEADS_PER_GROUP
                   - lax.broadcasted_iota(jnp.int32, (nr, n_new), 1))
        s_c = jnp.where((delta_c >= 0) & ((delta_c & (d - 1)) == 0), s_c, NEG_INF)
        s_n = jnp.where((delta_n >= 0) & ((delta_n & (d - 1)) == 0), s_n, NEG_INF)
        m = jnp.maximum(jnp.max(s_c, axis=-1, keepdims=True), jnp.max(s_n, axis=-1, keepdims=True))
        p_c = jnp.exp(s_c - m)
        p_n = jnp.exp(s_n - m)
        den = jnp.sum(p_c, axis=-1, keepdims=True) + jnp.sum(p_n, axis=-1, keepdims=True)
        pv = (lax.dot_general(p_c.astype(BF16), v_t, NT_DIMS, preferred_element_type=F32)
              + jnp.dot(p_n.astype(BF16), v_new, preferred_element_type=F32)) / den
        lse = jnp.broadcast_to(m + jnp.log(den), pv.shape)
        o_rows, l_rows = [], []
        for n in range(ns):
            blk = slice(n * HEADS_PER_GROUP, (n + 1) * HEADS_PER_GROUP)
            o_rows.append(jnp.sum(pv[blk] * head_mask, axis=0, keepdims=True))
            l_rows.append(jnp.sum(lse[blk] * head_mask, axis=0, keepdims=True))
        outs.append(jnp.concatenate(o_rows, axis=0))
        lses.append(jnp.concatenate(l_rows, axis=0))
    m = jnp.maximum(jnp.maximum(lses[0], lses[1]), lses[2])
    es = [jnp.exp(l - m) for l in lses]
    num = es[0] * outs[0] + es[1] * outs[1] + es[2] * outs[2]
    out_ref[...] = num / (es[0] + es[1] + es[2])


def _attn_sample(qkv, caches, new_caches, layer, dec_batch, ns):
    assert ns % 8 == 0
    in_specs = [pl.BlockSpec((ns, 3 * QKV_W), lambda b: (b, 0))]
    out_specs = [pl.BlockSpec((ns, GROUP_W), lambda b: (b, 0))]
    out_shape = [jax.ShapeDtypeStruct((dec_batch * ns, GROUP_W), F32)]
    for c in caches:
        spec = pl.BlockSpec((1, 1) + c.shape[2:], lambda b, layer=layer: (layer, b, 0, 0, 0))
        in_specs.append(spec)
        out_specs.append(spec)
        out_shape.append(jax.ShapeDtypeStruct(c.shape, F32))
    in_specs += [pl.BlockSpec(memory_space=pl.ANY)] * len(new_caches)
    n_in = 1 + len(caches)
    return pl.pallas_call(
        functools.partial(_attn_sample_kernel, ns),
        grid=(dec_batch,),
        in_specs=in_specs,
        out_specs=out_specs,
        out_shape=out_shape,
        input_output_aliases={n_in + g: 1 + g for g in range(len(new_caches))},
        compiler_params=_params("arbitrary"),
        name="attn_sample",
    )(qkv, *caches, *new_caches)


def _conv_kernel(tc, has_hist, *refs):
    if has_hist:
        c_ref, hist_ref, dw_ref, bias_ref, lng_ref, lnb_ref, out_ref, ext_ref = refs
        ext_ref[0:CONV_HIST, :] = hist_ref[0]
        ext_ref[CONV_HIST:, :] = c_ref[...]
    else:
        c_ref, dw_ref, bias_ref, lng_ref, lnb_ref, out_ref, ext_ref = refs
        i = pl.program_id(1)
        t0 = pl.multiple_of(i * tc, 8)
        prev = pl.multiple_of(jnp.maximum(t0 - CONV_HIST, 0), 8)
        hist = c_ref[pl.ds(prev, CONV_HIST), :]
        ext_ref[0:CONV_HIST, :] = jnp.where(i > 0, hist, 0.0)
        ext_ref[CONV_HIST:, :] = c_ref[pl.ds(t0, tc), :]
    off = CONV_HIST - (CONV_WIDTH - 1)
    acc = ext_ref[pl.ds(off, tc), :] * dw_ref[0:1, :]
    for k in range(1, CONV_WIDTH):
        acc = acc + ext_ref[pl.ds(off + k, tc), :] * dw_ref[k:k + 1, :]
    y = acc + bias_ref[...]
    mu = jnp.mean(y, axis=-1, keepdims=True)
    yc = y - mu
    var = jnp.mean(yc * yc, axis=-1, keepdims=True)
    y = yc * lax.rsqrt(var + NORM_EPS) * lng_ref[...] + lnb_ref[...]
    out_ref[...] = y * _sigmoid(y)


def _conv_prompt(c, dw, bias, lng, lnb, batch, seq, tc):
    assert seq % tc == 0 and tc % 8 == 0
    nt = seq // tc
    vec = lambda n: pl.BlockSpec((1, n), lambda b, i: (0, 0))
    return pl.pallas_call(
        functools.partial(_conv_kernel, tc, False),
        grid=(batch, nt),
        in_specs=[pl.BlockSpec((seq, GROUP_W), lambda b, i: (b, 0)),
                  pl.BlockSpec((CONV_WIDTH, GROUP_W), lambda b, i: (0, 0)), vec(GROUP_W), vec(GROUP_W), vec(GROUP_W)],
        out_specs=pl.BlockSpec((tc, GROUP_W), lambda b, i, nt=nt: (b * nt + i, 0)),
        out_shape=jax.ShapeDtypeStruct((batch * seq, GROUP_W), F32),
        scratch_shapes=[pltpu.VMEM((tc + CONV_HIST, GROUP_W), F32)],
        compiler_params=_params("parallel", "arbitrary"),
        name="conv_prompt",
    )(c, dw, bias, lng, lnb)


def _conv_sample(c, hist, dw, bias, lng, lnb, dec_batch, ns):
    vec = lambda n: pl.BlockSpec((1, n), lambda b: (0, 0))
    return pl.pallas_call(
        functools.partial(_conv_kernel, ns, True),
        grid=(dec_batch,),
        in_specs=[pl.BlockSpec((ns, GROUP_W), lambda b: (b, 0)),
                  pl.BlockSpec((1, CONV_HIST, GROUP_W), lambda b: (b, 0, 0)),
                  pl.BlockSpec((CONV_WIDTH, GROUP_W), lambda b: (0, 0)), vec(GROUP_W), vec(GROUP_W), vec(GROUP_W)],
        out_specs=pl.BlockSpec((ns, GROUP_W), lambda b: (b, 0)),
        out_shape=jax.ShapeDtypeStruct((dec_batch * ns, GROUP_W), F32),
        scratch_shapes=[pltpu.VMEM((ns + CONV_HIST, GROUP_W), F32)],
        compiler_params=_params("parallel"),
        name="conv_sample",
    )(c, hist, dw, bias, lng, lnb)


def _merge_kernel(x_ref, attn_ref, u_ref, ga_ref, gb_ref, wa_ref, wc_ref, bc_ref, wo_ref, out_ref):
    y_a = jnp.dot(attn_ref[...].astype(BF16), wa_ref[...], preferred_element_type=F32)
    y_b = jnp.dot(u_ref[...].astype(BF16), wc_ref[...], preferred_element_type=F32) + bc_ref[...]
    y = (ga_ref[...] * y_a + gb_ref[...] * y_b).astype(BF16)
    out_ref[...] = x_ref[...] + jnp.dot(y, wo_ref[...], preferred_element_type=F32)


def _merge(x, attn, u, ga, gb, wa, wc, bc, wo, *, tm):
    t, d_model = x.shape
    assert t % tm == 0
    row = lambda i: (i, 0)
    const = lambda i: (0, 0)
    return pl.pallas_call(
        _merge_kernel,
        grid=(t // tm,),
        in_specs=[pl.BlockSpec((tm, d_model), row), pl.BlockSpec((tm, GROUP_W), row), pl.BlockSpec((tm, GROUP_W), row),
                  pl.BlockSpec((tm, d_model), row), pl.BlockSpec((tm, d_model), row),
                  pl.BlockSpec((GROUP_W, d_model), const), pl.BlockSpec((GROUP_W, d_model), const),
                  pl.BlockSpec((1, d_model), const), pl.BlockSpec((d_model, d_model), const)],
        out_specs=pl.BlockSpec((tm, d_model), row),
        out_shape=jax.ShapeDtypeStruct((t, d_model), F32),
        compiler_params=_params("parallel"),
        name="merge",
    )(x, attn, u, ga, gb, wa, wc, bc, wo)


_NO_RANK = 64.0


def _top_values(s, out_ref, n, want_rank=False):
    cur = s
    rank = jnp.full(s.shape, _NO_RANK, F32) if want_rank else None
    for j in range(n):
        m = jnp.max(cur, axis=0, keepdims=True)
        out_ref[j:j + 1, :] = m
        hit = cur == m
        if want_rank:
            rank = jnp.where(hit, float(j), rank)
        if j + 1 < n:
            cur = jnp.where(hit, -jnp.inf, cur)
    return rank


def _split_bf16(x):
    hi = x.astype(BF16)
    return hi, (x - hi.astype(F32)).astype(BF16)


def _dot_bf16x3(a_hi, a_lo, b_hi, b_lo):
    d = functools.partial(jnp.dot, preferred_element_type=F32)
    return d(a_hi, b_hi) + (d(a_hi, b_lo) + d(a_lo, b_hi))


_CAND = [(i, PEER_TOPK // (i + 1)) for i in range(PEER_TOPK)]
_CAND_ROWS = 64
PEER_SLAB = 256
GATE_ROWS = 16


def _peer_kernel(tp, ech, lb, n_chunks, x_ref, g_ref, wqh_ref, wql_ref, kh_ref, kl_ref, u_ref, vt_ref, out_ref,
                 ht_ref, htl_ref, cnt_ref, e1_ref, sa_ref, sb_ref, rank_ref, e2_ref, at_ref, gt_ref, acc_ref,
                 a_ref, b_ref, cand_ref, top_ref):
    c = pl.program_id(1)
    n_lb = tp // lb

    @pl.when(c == 0)
    def _route():
        h2 = _rmsnorm(x_ref[...], g_ref[...])
        ht = h2.T
        ht_hi, ht_lo = _split_bf16(ht)
        ht_ref[:, :tp] = ht_hi
        htl_ref[:, :tp] = ht_lo
        acc_ref[:, :tp] = jnp.zeros((acc_ref.shape[0], tp), F32)
        gt_ref[1, :, :tp] = jnp.zeros((gt_ref.shape[1], tp), BF16)

        def scores(h, s_ref):
            for half in range(2):
                k = 2 * h + half
                rows = pl.ds(pl.multiple_of(k * N_KEYS, N_KEYS), N_KEYS)
                q = _dot_bf16x3(wqh_ref[rows, :], wql_ref[rows, :], ht_ref[:, :tp], htl_ref[:, :tp])
                q_hi, q_lo = _split_bf16(q)
                s_ref[half, :, :tp] = _dot_bf16x3(kh_ref[k], kl_ref[k], q_hi, q_lo)

        def select(h, s_ref):
            for j in range(n_lb):
                cols = slice(j * lb, (j + 1) * lb)
                s1 = s_ref[0, :, cols]
                s2 = s_ref[1, :, cols]
                _top_values(s1, a_ref, PEER_TOPK)
                rank2 = _top_values(s2, b_ref, PEER_TOPK, want_rank=True)
                cand_ref[...] = jnp.full(cand_ref.shape, -jnp.inf, F32)
                row = 0
                for (i, n) in _CAND:
                    cand_ref[row:row + n, :] = a_ref[i:i + 1, :] + b_ref[0:n, :]
                    row += n
                _top_values(cand_ref[...], top_ref, PEER_TOPK)
                top = top_ref[...]
                tau = top[PEER_TOPK - 1:PEER_TOPK, :]
                z = jnp.sum(jnp.exp(top - top[0:1, :]), axis=0, keepdims=True)
                a0, a15 = a_ref[0:1, :], a_ref[PEER_TOPK - 1:PEER_TOPK, :]
                b0 = b_ref[0:1, :]
                cnt = jnp.zeros(s1.shape, F32)
                for r in range(PEER_TOPK):
                    cnt = cnt + jnp.where(s1 + b_ref[r:r + 1, :] >= tau, 1.0, 0.0)
                cnt_ref[h, :, cols] = jnp.where(s1 >= a15, cnt, 0.0)
                e1_ref[h, :, cols] = jnp.exp(s1 - a0)
                rank_ref[h, :, cols] = rank2.astype(BF16)
                e2_ref[h, :, cols] = (jnp.exp(s2 - b0) / z).astype(BF16)

        scores(0, sa_ref)

        def head_pair(hp, carry):
            h = 2 * hp
            scores(h + 1, sb_ref)
            select(h, sa_ref)
            scores(jnp.minimum(h + 2, PEER_HEADS - 1), sa_ref)
            select(h + 1, sb_ref)
            return carry

        lax.fori_loop(0, PEER_HEADS // 2, head_pair, 0)

    n_slabs = ech // PEER_SLAB
    d_part = acc_ref.shape[0] // n_slabs

    cur = c % 2
    prev = 1 - cur

    def output_part(p):
        rows = slice(p * d_part, (p + 1) * d_part)
        acc_ref[rows, :tp] += jnp.dot(vt_ref[0, rows, :], gt_ref[prev, :, :tp], preferred_element_type=F32)

    def pre_activations(si):
        slab = slice(si * PEER_SLAB, (si + 1) * PEER_SLAB)
        at_ref[slab, :tp] = jnp.dot(u_ref[slab, :], ht_ref[:, :tp], preferred_element_type=F32)

    def gate_slab(si):
        i1_rows = pl.ds(pl.multiple_of(c * (ech // N_KEYS), 8), ech // N_KEYS)
        s0 = si * PEER_SLAB
        for i1l in range(s0 // N_KEYS, (s0 + PEER_SLAB) // N_KEYS):
            rows = slice(i1l * N_KEYS, (i1l + 1) * N_KEYS)
            for j in range(n_lb):
                cols = slice(j * lb, (j + 1) * lb)
                w = jnp.zeros((N_KEYS, lb), BF16)
                for h in range(PEER_HEADS):
                    cnt_b = jnp.broadcast_to(cnt_ref[h, i1_rows, cols][i1l:i1l + 1], (N_KEYS, lb)).astype(BF16)
                    e1_b = jnp.broadcast_to(e1_ref[h, i1_rows, cols][i1l:i1l + 1], (N_KEYS, lb)).astype(BF16)
                    keep = rank_ref[h, :, cols] < cnt_b
                    w = w + jnp.where(keep, e2_ref[h, :, cols] * e1_b, jnp.zeros((), BF16))
                a = at_ref[rows, cols]
                act = 0.5 * a * (1.0 + lax.erf(a * (0.5 ** 0.5)))
                gt_ref[cur, rows, cols] = w * act.astype(BF16)

    @pl.when(c < n_chunks)
    def _chunk():
        pre_activations(0)
        for si in range(n_slabs):
            if si + 1 < n_slabs:
                pre_activations(si + 1)
            output_part(si)
            gate_slab(si)

    @pl.when(c == n_chunks)
    def _drain():
        for p in range(n_slabs):
            output_part(p)
        out_ref[...] = x_ref[...] + acc_ref[:, :tp].T


PEER_CHUNK = 8 * N_KEYS
PEER_FLAGS = None


def _peer(x, g, wq_t, keys, u_bf16, vt_bf16, *, tp, lb=LANES):
    t, d_model = x.shape
    n_exp = u_bf16.shape[0]
    ech = PEER_CHUNK
    n_chunks = n_exp // ech
    assert t % tp == 0 and n_exp % ech == 0 and tp % lb == 0 and n_exp == N_KEYS * N_KEYS
    assert vt_bf16.shape == (n_chunks, d_model, ech)
    q_w = wq_t[0].shape[0]
    assert q_w == PEER_HEADS * 2 * N_KEYS and ech % PEER_SLAB == 0 and d_model % (ech // PEER_SLAB) == 0
    tpp = tp + LANES
    meta = lambda dt: pltpu.VMEM((PEER_HEADS, N_KEYS, tpp), dt)
    wq_spec = pl.BlockSpec((q_w, d_model), lambda i, c: (0, 0), pipeline_mode=pl.Buffered(1))
    keys_spec = pl.BlockSpec((2 * PEER_HEADS, N_KEYS, N_KEYS), lambda i, c: (0, 0, 0), pipeline_mode=pl.Buffered(1))
    return pl.pallas_call(
        functools.partial(_peer_kernel, tp, ech, lb, n_chunks),
        grid=(t // tp, n_chunks + 1),
        in_specs=[pl.BlockSpec((tp, d_model), lambda i, c: (i, 0)),
                  pl.BlockSpec((1, d_model), lambda i, c: (0, 0)),
                  wq_spec, wq_spec, keys_spec, keys_spec,
                  pl.BlockSpec((ech, d_model), lambda i, c: (jnp.minimum(c, n_chunks - 1), 0)),
                  pl.BlockSpec((1, d_model, ech), lambda i, c: (jnp.maximum(c - 1, 0), 0, 0))],
        out_specs=pl.BlockSpec((tp, d_model), lambda i, c: (i, 0)),
        out_shape=jax.ShapeDtypeStruct((t, d_model), F32),
        scratch_shapes=[pltpu.VMEM((d_model, tpp), BF16),
                        pltpu.VMEM((d_model, tpp), BF16),
                        meta(F32), meta(F32),
                        pltpu.VMEM((2, N_KEYS, tpp), F32), pltpu.VMEM((2, N_KEYS, tpp), F32),
                        meta(BF16), meta(BF16),
                        pltpu.VMEM((ech, tpp), F32),
                        pltpu.VMEM((2, ech, tpp), BF16),
                        pltpu.VMEM((d_model, tpp), F32),
                        pltpu.VMEM((PEER_TOPK, lb), F32), pltpu.VMEM((PEER_TOPK, lb), F32),
                        pltpu.VMEM((_CAND_ROWS, lb), F32), pltpu.VMEM((PEER_TOPK, lb), F32)],
        compiler_params=_params("parallel", "arbitrary", flags=PEER_FLAGS),
        name="peer",
    )(x, g, wq_t[0], wq_t[1], keys[0], keys[1], u_bf16, vt_bf16)


def _final_norm_kernel(x_ref, g_ref, out_ref):
    out_ref[...] = _rmsnorm(x_ref[...], g_ref[...])


def _final_norm(x, g, *, tm):
    t, d_model = x.shape
    return pl.pallas_call(
        _final_norm_kernel,
        grid=(t // tm,),
        in_specs=[pl.BlockSpec((tm, d_model), lambda i: (i, 0)), pl.BlockSpec((1, d_model), lambda i: (0, 0))],
        out_specs=pl.BlockSpec((tm, d_model), lambda i: (i, 0)),
        out_shape=jax.ShapeDtypeStruct((t, d_model), F32),
        compiler_params=_params("parallel"),
        name="final_norm",
    )(x, g)


def _time_last(buf):
    return buf.transpose(0, 1, 3, 4, 5, 2).reshape(buf.shape[0], buf.shape[1], 2, GROUP_W, buf.shape[2])


def _time_first(buf):
    depth, batch, _, _, time = buf.shape
    return buf.reshape(depth, batch, 2, HEADS_PER_GROUP, HEAD_DIM, time).transpose(0, 1, 5, 2, 3, 4)


def kernel(x_prompt, x_sample, cache_kv_w128, cache_kv_w512, cache_kv_w2048, state_conv, norm_mix_g, norm_ffn_g, w_in, dw_kernel, dw_bias, conv_ln_g, conv_ln_b, w_conv_out, b_conv_out, w_attn_out, w_out, w_peer_q, peer_sub_keys, peer_u, peer_v, norm_final_g):
    batch, seq, d_model = x_prompt.shape
    dec_batch, ns, _ = x_sample.shape
    depth = w_in.shape[0]
    tp_tokens, ts_tokens = batch * seq, dec_batch * ns
    tm_p = 512
    cache_in = (cache_kv_w128, cache_kv_w512, cache_kv_w2048)
    for c, (window, _) in zip(cache_in, ATTN_GROUPS):
        assert c.shape[2] == window
    caches = [_time_last(c) for c in cache_in]
    assert state_conv.shape[2] == CONV_WIDTH - 1
    hist_pad = CONV_HIST - (CONV_WIDTH - 1)

    xp = x_prompt.reshape(tp_tokens, d_model)
    xs = x_sample.reshape(ts_tokens, d_model)
    tails = [jnp.zeros((depth, batch, 2, GROUP_W, min(window, seq)), F32) for (window, _) in ATTN_GROUPS]
    new_caches = [jnp.zeros(c.shape, F32) for c in caches]
    conv_p, conv_s = [], []
    vec = lambda a: a.reshape(1, -1)
    for l in range(depth):
        w_in_b = w_in[l].astype(BF16)
        wa_b, wc_b, wo_b = w_attn_out[l].astype(BF16), w_conv_out[l].astype(BF16), w_out[l].astype(BF16)
        wq_t = _split_bf16(w_peer_q[l].T)
        keys = _split_bf16(peer_sub_keys[l].reshape(2 * PEER_HEADS, N_KEYS, -1))
        u_b = peer_u[l].astype(BF16)
        vt_b = peer_v[l].reshape(-1, PEER_CHUNK, d_model).transpose(0, 2, 1).astype(BF16)
        conv_w = (dw_kernel[l], vec(dw_bias[l]), vec(conv_ln_g[l]), vec(conv_ln_b[l]))
        mix_g, ffn_g = vec(norm_mix_g[l]), vec(norm_ffn_g[l])

        outs = _in_proj(xp, mix_g, w_in_b, tm=tm_p, class_major=True, batch=batch, seq=seq, layer=l, tails=tails)
        c, ga, gb = outs[:3]
        tails = list(outs[12:])
        attn = _attn_prompt(outs[3:12], batch, seq).reshape(tp_tokens, GROUP_W)
        u = _conv_prompt(c, *conv_w, batch, seq, 512)
        xp = _merge(xp, attn, u, ga, gb, wa_b, wc_b, vec(b_conv_out[l]), wo_b, tm=tm_p)
        xp = _peer(xp, ffn_g, wq_t, keys, u_b, vt_b, tp=512)
        conv_p.append(c.reshape(batch, seq, GROUP_W)[:, seq - (CONV_WIDTH - 1):])

        qkv, c, ga, gb = _in_proj(xs, mix_g, w_in_b, tm=ts_tokens, class_major=False)
        attn, *new_caches = _attn_sample(qkv, caches, new_caches, l, dec_batch, ns)
        hist = jnp.pad(state_conv[l], ((0, 0), (hist_pad, 0), (0, 0)))
        u = _conv_sample(c, hist, *conv_w, dec_batch, ns)
        xs = _merge(xs, attn, u, ga, gb, wa_b, wc_b, vec(b_conv_out[l]), wo_b, tm=ts_tokens)
        xs = _peer(xs, ffn_g, wq_t, keys, u_b, vt_b, tp=ts_tokens)
        full = jnp.concatenate([state_conv[l], c.reshape(dec_batch, ns, GROUP_W)], axis=1)
        conv_s.append(full[:, full.shape[1] - (CONV_WIDTH - 1):])

    y_prompt = _final_norm(xp, vec(norm_final_g), tm=tm_p).reshape(batch, seq, d_model)
    y_sample = _final_norm(xs, vec(norm_final_g), tm=ts_tokens).reshape(dec_batch, ns, d_model)
    return (y_prompt, y_sample,
            _time_first(tails[0]), _time_first(tails[1]), _time_first(tails[2]), jnp.stack(conv_p),
            _time_first(new_caches[0]), _time_first(new_caches[1]), _time_first(new_caches[2]), jnp.stack(conv_s))
```

```python
import functools

import jax
import jax.numpy as jnp
from jax import lax
from jax.experimental import pallas as pl
from jax.experimental.pallas import tpu as pltpu

F32 = jnp.float32
BF16 = jnp.bfloat16

HEAD_DIM = 64
HEADS_PER_GROUP = 8
ATTN_GROUPS = ((128, 1), (512, 4), (2048, 16))
N_GROUPS = len(ATTN_GROUPS)
GROUP_W = HEADS_PER_GROUP * HEAD_DIM
QKV_W = N_GROUPS * GROUP_W
ATTN_BLOCK = 128
ATTN_SCALE = HEAD_DIM ** -0.5
CONV_WIDTH = 31
CONV_HIST = 32
N_KEYS = 128
PEER_HEADS = 8
PEER_TOPK = 16
NORM_EPS = 1e-6
NEG_INF = -1e30

VMEM_LIMIT_BYTES = 56 * 1024 * 1024
LANES = 128
HEAD_HALF = 4
ATTN_UNROLL = 4
HALF_W = HEAD_HALF * HEAD_DIM

NT_DIMS = (((1,), (1,)), ((), ()))


def _params(*sem, flags=None):
    return pltpu.CompilerParams(dimension_semantics=sem, vmem_limit_bytes=VMEM_LIMIT_BYTES, flags=flags)


def _rmsnorm(x, g):
    return x * lax.rsqrt(jnp.mean(x * x, axis=-1, keepdims=True) + NORM_EPS) * g


def _sigmoid(x):
    return 1.0 / (1.0 + jnp.exp(-x))


def _in_proj_kernel(class_major, tm, seq, x_ref, g_ref, w_ref, *refs):
    if class_major:
        c_ref, ga_ref, gb_ref = refs[3:6]
        cm_refs = refs[6:15]
        tail_refs = refs[15:18]
        scr_ref = refs[18]
        tps = seq // tm
        last_tile = pl.program_id(0) % tps == tps - 1
    else:
        qkv_ref, c_ref, ga_ref, gb_ref = refs[:4]
    h = _rmsnorm(x_ref[...], g_ref[...]).astype(BF16)
    glu_a = None
    for j in range(w_ref.shape[1] // GROUP_W):
        res = jnp.dot(h, w_ref[:, j * GROUP_W:(j + 1) * GROUP_W], preferred_element_type=F32)
        if j < 9:
            kind, g = divmod(j, N_GROUPS)
            if class_major:
                if kind > 0:
                    keep = tail_refs[g].shape[-1]
                    if min(ATTN_GROUPS[g][0], seq) == seq:
                        tail_refs[g][0, 0, kind - 1] = res.T
                    else:
                        @pl.when(last_tile)
                        def _(res=res, g=g, kind=kind, keep=keep):
                            tail_refs[g][0, 0, kind - 1] = res[tm - keep:, :].T
                val = res * ATTN_SCALE if kind == 0 else res
                d = ATTN_GROUPS[g][1]
                cm = cm_refs[kind * N_GROUPS + g]
                if d == 1:
                    cm[0, 0] = val.astype(BF16)
                else:
                    for t in range(GROUP_W // LANES):
                        scr_ref[t] = val[:, t * LANES:(t + 1) * LANES]
                    for r in range(d):
                        cm[0, r] = jnp.concatenate(
                            [scr_ref[t, pl.ds(r, tm // d, stride=d), :] for t in range(GROUP_W // LANES)], axis=1).astype(BF16)
            else:
                qkv_ref[:, j * GROUP_W:(j + 1) * GROUP_W] = res
        elif j == 9:
            glu_a = res
        elif j == 10:
            c_ref[...] = glu_a * _sigmoid(res)
        elif j < 13:
            ga_ref[:, (j - 11) * GROUP_W:(j - 10) * GROUP_W] = _sigmoid(res)
        else:
            gb_ref[:, (j - 13) * GROUP_W:(j - 12) * GROUP_W] = _sigmoid(res)


def _in_proj(x, g, w_bf16, *, tm, class_major, batch=None, seq=None, layer=None, tails=None):
    t, d_model = x.shape
    in_w = w_bf16.shape[1]
    assert t % tm == 0 and in_w == 3 * QKV_W + 2 * GROUP_W + 2 * d_model and d_model == 2 * GROUP_W
    nt = t // tm
    row = lambda i: (i, 0)
    in_specs = [
        pl.BlockSpec((tm, d_model), row),
        pl.BlockSpec((1, d_model), lambda i: (0, 0)),
        pl.BlockSpec((d_model, in_w), lambda i: (0, 0), pipeline_mode=pl.Buffered(1)),
    ]
    common_shapes = [
        jax.ShapeDtypeStruct((t, GROUP_W), F32),
        jax.ShapeDtypeStruct((t, d_model), F32),
        jax.ShapeDtypeStruct((t, d_model), F32),
    ]
    common_specs = [pl.BlockSpec((tm, GROUP_W), row), pl.BlockSpec((tm, d_model), row), pl.BlockSpec((tm, d_model), row)]
    aliases = {}
    args = (x, g, w_bf16)
    if class_major:
        assert seq % tm == 0
        tps = seq // tm
        in_specs += [pl.BlockSpec(memory_space=pl.ANY)] * N_GROUPS
        args += tuple(tails)
        out_shape = list(common_shapes)
        out_specs = list(common_specs)
        for _kind in range(3):
            for (_, d) in ATTN_GROUPS:
                assert tm % (d * 16) == 0
                out_shape.append(jax.ShapeDtypeStruct((batch, d, seq // d, GROUP_W), BF16))
                out_specs.append(pl.BlockSpec((1, d, tm // d, GROUP_W), lambda i, tps=tps: (i // tps, 0, i % tps, 0)))
        for g_idx, ((window, _), tail) in enumerate(zip(ATTN_GROUPS, tails)):
            keep = min(window, seq)
            assert tail.shape[1:] == (batch, 2, GROUP_W, keep) and (keep == seq or (keep <= tm and keep % LANES == 0))
            aliases[3 + g_idx] = len(out_shape)
            out_shape.append(jax.ShapeDtypeStruct(tail.shape, F32))
            if keep == seq:
                out_specs.append(pl.BlockSpec((1, 1, 2, GROUP_W, tm), lambda i, tps=tps: (layer, i // tps, 0, 0, i % tps)))
            else:
                out_specs.append(pl.BlockSpec((1, 1, 2, GROUP_W, keep), lambda i, tps=tps: (layer, i // tps, 0, 0, 0)))
        scratch = [pltpu.VMEM((GROUP_W // LANES, tm, LANES), F32)]
    else:
        out_shape = [jax.ShapeDtypeStruct((t, 3 * QKV_W), F32)] + common_shapes
        out_specs = [pl.BlockSpec((tm, 3 * QKV_W), row)] + common_specs
        scratch = []
    return pl.pallas_call(
        functools.partial(_in_proj_kernel, class_major, tm, seq),
        grid=(nt,),
        in_specs=in_specs,
        out_specs=out_specs,
        out_shape=out_shape,
        scratch_shapes=scratch,
        input_output_aliases=aliases,
        compiler_params=_params("arbitrary"),
        name="in_proj_cm" if class_major else "in_proj_nat",
    )(*args)


def _attend_heads(qb, kb, vb, mask, n_heads):
    outs, lses = [], []
    for h in range(n_heads):
        sl = slice(h * HEAD_DIM, (h + 1) * HEAD_DIM)
        s = lax.dot_general(qb[:, sl], kb[:, sl], NT_DIMS, preferred_element_type=F32)
        s = jnp.where(mask, s, NEG_INF)
        m = jnp.max(s, axis=-1, keepdims=True)
        p = jnp.exp(s - m)
        den = jnp.sum(p, axis=-1, keepdims=True)
        o = jnp.dot(p.astype(BF16), vb[:, sl], preferred_element_type=F32) / den
        outs.append(o)
        lses.append(jnp.broadcast_to(m + jnp.log(den), o.shape))
    return jnp.concatenate(outs, axis=1), jnp.concatenate(lses, axis=1)


def _attn_prompt_kernel(seq, *refs):
    qkv_refs = refs[:9]
    out_ref = refs[9]
    ocm_ref, lcm_ref, m_ref, n_ref, d_ref = refs[10:]
    n_lt = HALF_W // LANES
    dist = (lax.broadcasted_iota(jnp.int32, (ATTN_BLOCK, 2 * ATTN_BLOCK), 0) + ATTN_BLOCK
            - lax.broadcasted_iota(jnp.int32, (ATTN_BLOCK, 2 * ATTN_BLOCK), 1))
    band_mask = (dist >= 0) & (dist <= ATTN_BLOCK)
    causal_mask = (lax.broadcasted_iota(jnp.int32, (ATTN_BLOCK, ATTN_BLOCK), 0)
                   >= lax.broadcasted_iota(jnp.int32, (ATTN_BLOCK, ATTN_BLOCK), 1))

    def put(ref, rows, val):
        for t in range(n_lt):
            ref[t, rows, :] = val[:, t * LANES:(t + 1) * LANES]

    for g, (window, d) in enumerate(ATTN_GROUPS):
        assert window // d == ATTN_BLOCK
        q_ref, k_ref, v_ref = qkv_refs[g], qkv_refs[N_GROUPS + g], qkv_refs[2 * N_GROUPS + g]
        sd = seq // d
        nb = sd // ATTN_BLOCK

        def block(r, i, q_ref=q_ref, k_ref=k_ref, v_ref=v_ref, sd=sd):
            if isinstance(i, int) and i == 0:
                q0 = 0
                o, l = _attend_heads(q_ref[0, r, 0:ATTN_BLOCK, :], k_ref[0, r, 0:ATTN_BLOCK, :],
                                     v_ref[0, r, 0:ATTN_BLOCK, :], causal_mask, HEAD_HALF)
            else:
                q0, k0 = i * ATTN_BLOCK, (i - 1) * ATTN_BLOCK
                if not isinstance(i, int):
                    q0, k0 = pl.multiple_of(q0, ATTN_BLOCK), pl.multiple_of(k0, ATTN_BLOCK)
                o, l = _attend_heads(q_ref[0, r, pl.ds(q0, ATTN_BLOCK), :], k_ref[0, r, pl.ds(k0, 2 * ATTN_BLOCK), :],
                                     v_ref[0, r, pl.ds(k0, 2 * ATTN_BLOCK), :], band_mask, HEAD_HALF)
            dst = r * sd + q0
            if not isinstance(dst, int):
                dst = pl.multiple_of(dst, ATTN_BLOCK)
            put(ocm_ref, pl.ds(dst, ATTN_BLOCK), o)
            put(lcm_ref, pl.ds(dst, ATTN_BLOCK), l)

        if nb == 1:
            assert d % ATTN_UNROLL == 0

            def classes(it, carry, block=block):
                for u in range(ATTN_UNROLL):
                    block(it * ATTN_UNROLL + u, 0)
                return carry

            lax.fori_loop(0, d // ATTN_UNROLL, classes, 0)
        else:
            assert nb % ATTN_UNROLL == 0

            def one_class(r, carry, block=block, nb=nb):
                for i in range(ATTN_UNROLL):
                    block(r, i)

                def later_blocks(it, c2):
                    for u in range(ATTN_UNROLL):
                        block(r, (it + 1) * ATTN_UNROLL + u)
                    return c2

                if nb > ATTN_UNROLL:
                    lax.fori_loop(0, nb // ATTN_UNROLL - 1, later_blocks, 0)
                return carry

            if d == 1:
                one_class(0, 0)
            else:
                lax.fori_loop(0, d, one_class, 0)

        if g == 0:
            assert d == 1
            m_ref[...] = lcm_ref[...]
            n_ref[...] = ocm_ref[...]
            d_ref[...] = jnp.ones(d_ref.shape, F32)
        else:
            for r in range(d):
                rows = pl.ds(r, sd, stride=d)
                for t in range(n_lt):
                    o_r = ocm_ref[t, r * sd:(r + 1) * sd, :]
                    l_r = lcm_ref[t, r * sd:(r + 1) * sd, :]
                    m_old = m_ref[t, rows, :]
                    m_new = jnp.maximum(m_old, l_r)
                    a = jnp.exp(m_old - m_new)
                    b = jnp.exp(l_r - m_new)
                    n_ref[t, rows, :] = n_ref[t, rows, :] * a + o_r * b
                    d_ref[t, rows, :] = d_ref[t, rows, :] * a + b
                    m_ref[t, rows, :] = m_new
    for t in range(n_lt):
        out_ref[0, :, t * LANES:(t + 1) * LANES] = (n_ref[t] / d_ref[t]).astype(BF16)


def _attn_prompt(cm, batch, seq):
    in_specs = []
    for _kind in range(3):
        for (_, d) in ATTN_GROUPS:
            assert seq % (d * ATTN_BLOCK) == 0
            in_specs.append(pl.BlockSpec((1, d, seq // d, HALF_W), lambda b, hh: (b, 0, 0, hh)))
    return pl.pallas_call(
        functools.partial(_attn_prompt_kernel, seq),
        grid=(batch, GROUP_W // HALF_W),
        in_specs=in_specs,
        out_specs=pl.BlockSpec((1, seq, HALF_W), lambda b, hh: (b, 0, hh)),
        out_shape=jax.ShapeDtypeStruct((batch, seq, GROUP_W), BF16),
        scratch_shapes=[pltpu.VMEM((HALF_W // LANES, seq, LANES), F32) for _ in range(5)],
        compiler_params=_params("parallel", "parallel"),
        name="attn_prompt",
    )(*cm)


def _attn_sample_kernel(ns, qkv_ref, c0_ref, c1_ref, c2_ref, _n0, _n1, _n2, out_ref, new0_ref, new1_ref, new2_ref):
    cache_refs = (c0_ref, c1_ref, c2_ref)
    new_refs = (new0_ref, new1_ref, new2_ref)
    qkv = qkv_ref[...]
    row_pad = jnp.zeros((LANES - ns, GROUP_W), F32)
    lane = lax.broadcasted_iota(jnp.int32, (HEADS_PER_GROUP, GROUP_W), 1)
    hrow = lax.broadcasted_iota(jnp.int32, (HEADS_PER_GROUP, GROUP_W), 0)
    head_mask = (lane // HEAD_DIM == hrow).astype(F32)
    nr = ns * HEADS_PER_GROUP
    n_new = 16
    assert ns <= n_new
    zpad = jnp.zeros((n_new - ns, GROUP_W), F32)
    outs, lses = [], []
    for g, (window, d) in enumerate(ATTN_GROUPS):
        assert window % d == 0 and d & (d - 1) == 0 and cache_refs[g].shape[2:] == (2, GROUP_W, window)
        k_t = cache_refs[g][0, 0, 0].astype(BF16)
        v_t = cache_refs[g][0, 0, 1].astype(BF16)
        q = qkv[:, g * GROUP_W:(g + 1) * GROUP_W] * ATTN_SCALE
        k_new32 = qkv[:, QKV_W + g * GROUP_W:QKV_W + (g + 1) * GROUP_W]
        v_new32 = qkv[:, 2 * QKV_W + g * GROUP_W:2 * QKV_W + (g + 1) * GROUP_W]
        for kv, new32 in enumerate((k_new32, v_new32)):
            new_t = jnp.concatenate([new32, row_pad], axis=0).T
            new_refs[g][0, 0, kv] = jnp.concatenate([cache_refs[g][0, 0, kv][:, ns:], new_t[:, :ns]], axis=1)
        k_new = jnp.concatenate([k_new32, zpad], axis=0).astype(BF16)
        v_new = jnp.concatenate([v_new32, zpad], axis=0).astype(BF16)
        q_exp = jnp.concatenate([q[n:n + 1] * head_mask for n in range(ns)], axis=0).astype(BF16)
        s_c = jnp.dot(q_exp, k_t, preferred_element_type=F32)
        s_n = lax.dot_general(q_exp, k_new, NT_DIMS, preferred_element_type=F32)
        delta_c = (lax.broadcasted_iota(jnp.int32, (nr, window), 1)
                   - lax.broadcasted_iota(jnp.int32, (nr, window), 0) // HEADS_PER_GROUP)
        delta_n = (lax.broadcasted_iota(jnp.int32, (nr, n_new), 0) // HEADS_PER_GROUP
                   - lax.broadcasted_iota(jnp.int32, (nr, n_new), 1))
        s_c = jnp.where((delta_c >= 0) & ((delta_c & (d - 1)) == 0), s_c, NEG_INF)
        s_n = jnp.where((delta_n >= 0) & ((delta_n & (d - 1)) == 0), s_n, NEG_INF)
        m = jnp.maximum(jnp.max(s_c, axis=-1, keepdims=True), jnp.max(s_n, axis=-1, keepdims=True))
        p_c = jnp.exp(s_c - m)
        p_n = jnp.exp(s_n - m)
        den = jnp.sum(p_c, axis=-1, keepdims=True) + jnp.sum(p_n, axis=-1, keepdims=True)
        pv = (lax.dot_general(p_c.astype(BF16), v_t, NT_DIMS, preferred_element_type=F32)
              + jnp.dot(p_n.astype(BF16), v_new, preferred_element_type=F32)) / den
        lse = jnp.broadcast_to(m + jnp.log(den), pv.shape)
        o_rows, l_rows = [], []
        for n in range(ns):
            blk = slice(n * HEADS_PER_GROUP, (n + 1) * HEADS_PER_GROUP)
            o_rows.append(jnp.sum(pv[blk] * head_mask, axis=0, keepdims=True))
            l_rows.append(jnp.sum(lse[blk] * head_mask, axis=0, keepdims=True))
        outs.append(jnp.concatenate(o_rows, axis=0))
        lses.append(jnp.concatenate(l_rows, axis=0))
    m = jnp.maximum(jnp.maximum(lses[0], lses[1]), lses[2])
    es = [jnp.exp(l - m) for l in lses]
    num = es[0] * outs[0] + es[1] * outs[1] + es[2] * outs[2]
    out_ref[...] = num / (es[0] + es[1] + es[2])


def _attn_sample(qkv, caches, new_caches, layer, dec_batch, ns):
    assert ns % 8 == 0
    in_specs = [pl.BlockSpec((ns, 3 * QKV_W), lambda b: (b, 0))]
    out_specs = [pl.BlockSpec((ns, GROUP_W), lambda b: (b, 0))]
    out_shape = [jax.ShapeDtypeStruct((dec_batch * ns, GROUP_W), F32)]
    for c in caches:
        spec = pl.BlockSpec((1, 1) + c.shape[2:], lambda b, layer=layer: (layer, b, 0, 0, 0))
        in_specs.append(spec)
        out_specs.append(spec)
        out_shape.append(jax.ShapeDtypeStruct(c.shape, F32))
    in_specs += [pl.BlockSpec(memory_space=pl.ANY)] * len(new_caches)
    n_in = 1 + len(caches)
    return pl.pallas_call(
        functools.partial(_attn_sample_kernel, ns),
        grid=(dec_batch,),
        in_specs=in_specs,
        out_specs=out_specs,
        out_shape=out_shape,
        input_output_aliases={n_in + g: 1 + g for g in range(len(new_caches))},
        compiler_params=_params("arbitrary"),
        name="attn_sample",
    )(qkv, *caches, *new_caches)


def _conv_kernel(tc, has_hist, *refs):
    if has_hist:
        c_ref, hist_ref, dw_ref, bias_ref, lng_ref, lnb_ref, out_ref, ext_ref = refs
        ext_ref[0:CONV_HIST, :] = hist_ref[0]
        ext_ref[CONV_HIST:, :] = c_ref[...]
    else:
        c_ref, dw_ref, bias_ref, lng_ref, lnb_ref, out_ref, ext_ref = refs
        i = pl.program_id(1)
        t0 = pl.multiple_of(i * tc, 8)
        prev = pl.multiple_of(jnp.maximum(t0 - CONV_HIST, 0), 8)
        hist = c_ref[pl.ds(prev, CONV_HIST), :]
        ext_ref[0:CONV_HIST, :] = jnp.where(i > 0, hist, 0.0)
        ext_ref[CONV_HIST:, :] = c_ref[pl.ds(t0, tc), :]
    off = CONV_HIST - (CONV_WIDTH - 1)
    acc = ext_ref[pl.ds(off, tc), :] * dw_ref[0:1, :]
    for k in range(1, CONV_WIDTH):
        acc = acc + ext_ref[pl.ds(off + k, tc), :] * dw_ref[k:k + 1, :]
    y = acc + bias_ref[...]
    mu = jnp.mean(y, axis=-1, keepdims=True)
    yc = y - mu
    var = jnp.mean(yc * yc, axis=-1, keepdims=True)
    y = yc * lax.rsqrt(var + NORM_EPS) * lng_ref[...] + lnb_ref[...]
    out_ref[...] = y * _sigmoid(y)


def _conv_prompt(c, dw, bias, lng, lnb, batch, seq, tc):
    assert seq % tc == 0 and tc % 8 == 0
    nt = seq // tc
    vec = lambda n: pl.BlockSpec((1, n), lambda b, i: (0, 0))
    return pl.pallas_call(
        functools.partial(_conv_kernel, tc, False),
        grid=(batch, nt),
        in_specs=[pl.BlockSpec((seq, GROUP_W), lambda b, i: (b, 0)),
                  pl.BlockSpec((CONV_WIDTH, GROUP_W), lambda b, i: (0, 0)), vec(GROUP_W), vec(GROUP_W), vec(GROUP_W)],
        out_specs=pl.BlockSpec((tc, GROUP_W), lambda b, i, nt=nt: (b * nt + i, 0)),
        out_shape=jax.ShapeDtypeStruct((batch * seq, GROUP_W), F32),
        scratch_shapes=[pltpu.VMEM((tc + CONV_HIST, GROUP_W), F32)],
        compiler_params=_params("parallel", "arbitrary"),
        name="conv_prompt",
    )(c, dw, bias, lng, lnb)


def _conv_sample(c, hist, dw, bias, lng, lnb, dec_batch, ns):
    vec = lambda n: pl.BlockSpec((1, n), lambda b: (0, 0))
    return pl.pallas_call(
        functools.partial(_conv_kernel, ns, True),
        grid=(dec_batch,),
        in_specs=[pl.BlockSpec((ns, GROUP_W), lambda b: (b, 0)),
                  pl.BlockSpec((1, CONV_HIST, GROUP_W), lambda b: (b, 0, 0)),
                  pl.BlockSpec((CONV_WIDTH, GROUP_W), lambda b: (0, 0)), vec(GROUP_W), vec(GROUP_W), vec(GROUP_W)],
        out_specs=pl.BlockSpec((ns, GROUP_W), lambda b: (b, 0)),
        out_shape=jax.ShapeDtypeStruct((dec_batch * ns, GROUP_W), F32),
        scratch_shapes=[pltpu.VMEM((ns + CONV_HIST, GROUP_W), F32)],
        compiler_params=_params("parallel"),
        name="conv_sample",
    )(c, hist, dw, bias, lng, lnb)


def _merge_kernel(x_ref, attn_ref, u_ref, ga_ref, gb_ref, wa_ref, wc_ref, bc_ref, wo_ref, out_ref):
    y_a = jnp.dot(attn_ref[...].astype(BF16), wa_ref[...], preferred_element_type=F32)
    y_b = jnp.dot(u_ref[...].astype(BF16), wc_ref[...], preferred_element_type=F32) + bc_ref[...]
    y = (ga_ref[...] * y_a + gb_ref[...] * y_b).astype(BF16)
    out_ref[...] = x_ref[...] + jnp.dot(y, wo_ref[...], preferred_element_type=F32)


def _merge(x, attn, u, ga, gb, wa, wc, bc, wo, *, tm):
    t, d_model = x.shape
    assert t % tm == 0
    row = lambda i: (i, 0)
    const = lambda i: (0, 0)
    return pl.pallas_call(
        _merge_kernel,
        grid=(t // tm,),
        in_specs=[pl.BlockSpec((tm, d_model), row), pl.BlockSpec((tm, GROUP_W), row), pl.BlockSpec((tm, GROUP_W), row),
                  pl.BlockSpec((tm, d_model), row), pl.BlockSpec((tm, d_model), row),
                  pl.BlockSpec((GROUP_W, d_model), const), pl.BlockSpec((GROUP_W, d_model), const),
                  pl.BlockSpec((1, d_model), const), pl.BlockSpec((d_model, d_model), const)],
        out_specs=pl.BlockSpec((tm, d_model), row),
        out_shape=jax.ShapeDtypeStruct((t, d_model), F32),
        compiler_params=_params("parallel"),
        name="merge",
    )(x, attn, u, ga, gb, wa, wc, bc, wo)


_NO_RANK = 64.0


def _top_values(s, out_ref, n, want_rank=False):
    cur = s
    rank = jnp.full(s.shape, _NO_RANK, F32) if want_rank else None
    for j in range(n):
        m = jnp.max(cur, axis=0, keepdims=True)
        out_ref[j:j + 1, :] = m
        hit = cur == m
        if want_rank:
            rank = jnp.where(hit, float(j), rank)
        if j + 1 < n:
            cur = jnp.where(hit, -jnp.inf, cur)
    return rank


def _split_bf16(x):
    hi = x.astype(BF16)
    return hi, (x - hi.astype(F32)).astype(BF16)


def _dot_bf16x3(a_hi, a_lo, b_hi, b_lo):
    d = functools.partial(jnp.dot, preferred_element_type=F32)
    return d(a_hi, b_hi) + (d(a_hi, b_lo) + d(a_lo, b_hi))


_CAND = [(i, PEER_TOPK // (i + 1)) for i in range(PEER_TOPK)]
_CAND_ROWS = 64
PEER_SLAB = 512
GATE_ROWS = 16


def _peer_kernel(tp, ech, lb, n_chunks, x_ref, g_ref, wqh_ref, wql_ref, kh_ref, kl_ref, u_ref, vt_ref, out_ref,
                 ht_ref, htl_ref, cnt_ref, e1_ref, sa_ref, sb_ref, rank_ref, e2_ref, at_ref, gt_ref, acc_ref,
                 a_ref, b_ref, cand_ref, top_ref):
    c = pl.program_id(1)
    n_lb = tp // lb

    @pl.when(c == 0)
    def _route():
        h2 = _rmsnorm(x_ref[...], g_ref[...])
        ht = h2.T
        ht_hi, ht_lo = _split_bf16(ht)
        ht_ref[:, :tp] = ht_hi
        htl_ref[:, :tp] = ht_lo
        acc_ref[:, :tp] = jnp.zeros((acc_ref.shape[0], tp), F32)
        gt_ref[1, :, :tp] = jnp.zeros((gt_ref.shape[1], tp), BF16)

        def scores(h, s_ref):
            rows = pl.ds(pl.multiple_of(2 * h * N_KEYS, 2 * N_KEYS), 2 * N_KEYS)
            q = _dot_bf16x3(wqh_ref[rows, :], wql_ref[rows, :], ht_ref[:, :tp], htl_ref[:, :tp])
            q_hi, q_lo = _split_bf16(q)
            for half in range(2):
                k = 2 * h + half
                sl = slice(half * N_KEYS, (half + 1) * N_KEYS)
                s_ref[half, :, :tp] = _dot_bf16x3(kh_ref[k], kl_ref[k], q_hi[sl], q_lo[sl])

        def select(h, s_ref):
            for j in range(n_lb):
                cols = slice(j * lb, (j + 1) * lb)
                s1 = s_ref[0, :, cols]
                s2 = s_ref[1, :, cols]
                _top_values(s1, a_ref, PEER_TOPK)
                rank2 = _top_values(s2, b_ref, PEER_TOPK, want_rank=True)
                cand_ref[...] = jnp.full(cand_ref.shape, -jnp.inf, F32)
                row = 0
                for (i, n) in _CAND:
                    cand_ref[row:row + n, :] = a_ref[i:i + 1, :] + b_ref[0:n, :]
                    row += n
                _top_values(cand_ref[...], top_ref, PEER_TOPK)
                top = top_ref[...]
                tau = top[PEER_TOPK - 1:PEER_TOPK, :]
                z = jnp.sum(jnp.exp(top - top[0:1, :]), axis=0, keepdims=True)
                a0, a15 = a_ref[0:1, :], a_ref[PEER_TOPK - 1:PEER_TOPK, :]
                b0 = b_ref[0:1, :]
                cnt = jnp.zeros(s1.shape, F32)
                for r in range(PEER_TOPK):
                    cnt = cnt + jnp.where(s1 + b_ref[r:r + 1, :] >= tau, 1.0, 0.0)
                cnt_ref[h, :, cols] = jnp.where(s1 >= a15, cnt, 0.0)
                e1_ref[h, :, cols] = jnp.exp(s1 - a0)
                rank_ref[h, :, cols] = rank2.astype(BF16)
                e2_ref[h, :, cols] = (jnp.exp(s2 - b0) / z).astype(BF16)

        scores(0, sa_ref)

        def head_pair(hp, carry):
            h = 2 * hp
            scores(h + 1, sb_ref)
            select(h, sa_ref)
            scores(jnp.minimum(h + 2, PEER_HEADS - 1), sa_ref)
            select(h + 1, sb_ref)
            return carry

        lax.fori_loop(0, PEER_HEADS // 2, head_pair, 0)

    n_slabs = ech // PEER_SLAB
    d_part = acc_ref.shape[0] // n_slabs

    cur = c % 2
    prev = 1 - cur

    def output_part(p):
        rows = slice(p * d_part, (p + 1) * d_part)
        acc_ref[rows, :tp] += jnp.dot(vt_ref[0, rows, :], gt_ref[prev, :, :tp], preferred_element_type=F32)

    def pre_activations(si):
        slab = slice(si * PEER_SLAB, (si + 1) * PEER_SLAB)
        at_ref[slab, :tp] = jnp.dot(u_ref[slab, :], ht_ref[:, :tp], preferred_element_type=F32)

    def gate_slab(si):
        i1_rows = pl.ds(pl.multiple_of(c * (ech // N_KEYS), 8), ech // N_KEYS)
        s0 = si * PEER_SLAB
        for i1l in range(s0 // N_KEYS, (s0 + PEER_SLAB) // N_KEYS):
            rows = slice(i1l * N_KEYS, (i1l + 1) * N_KEYS)
            for j in range(n_lb):
                cols = slice(j * lb, (j + 1) * lb)
                w = jnp.zeros((N_KEYS, lb), BF16)
                for h in range(PEER_HEADS):
                    cnt_b = jnp.broadcast_to(cnt_ref[h, i1_rows, cols][i1l:i1l + 1], (N_KEYS, lb)).astype(BF16)
                    e1_b = jnp.broadcast_to(e1_ref[h, i1_rows, cols][i1l:i1l + 1], (N_KEYS, lb)).astype(BF16)
                    keep = rank_ref[h, :, cols] < cnt_b
                    w = w + jnp.where(keep, e2_ref[h, :, cols] * e1_b, jnp.zeros((), BF16))
                a = at_ref[rows, cols]
                act = 0.5 * a * (1.0 + lax.erf(a * (0.5 ** 0.5)))
                gt_ref[cur, rows, cols] = w * act.astype(BF16)

    @pl.when(c < n_chunks)
    def _chunk():
        pre_activations(0)
        for si in range(n_slabs):
            if si + 1 < n_slabs:
                pre_activations(si + 1)
            output_part(si)
            gate_slab(si)

    @pl.when(c == n_chunks)
    def _drain():
        for p in range(n_slabs):
            output_part(p)
        out_ref[...] = x_ref[...] + acc_ref[:, :tp].T


PEER_CHUNK = 8 * N_KEYS
PEER_FLAGS = None


def _peer(x, g, wq_t, keys, u_bf16, vt_bf16, *, tp, lb=LANES):
    t, d_model = x.shape
    n_exp = u_bf16.shape[0]
    ech = PEER_CHUNK
    n_chunks = n_exp // ech
    assert t % tp == 0 and n_exp % ech == 0 and tp % lb == 0 and n_exp == N_KEYS * N_KEYS
    assert vt_bf16.shape == (n_chunks, d_model, ech)
    q_w = wq_t[0].shape[0]
    assert q_w == PEER_HEADS * 2 * N_KEYS and ech % PEER_SLAB == 0 and d_model % (ech // PEER_SLAB) == 0
    tpp = tp + LANES
    meta = lambda dt: pltpu.VMEM((PEER_HEADS, N_KEYS, tpp), dt)
    wq_spec = pl.BlockSpec((q_w, d_model), lambda i, c: (0, 0), pipeline_mode=pl.Buffered(1))
    keys_spec = pl.BlockSpec((2 * PEER_HEADS, N_KEYS, N_KEYS), lambda i, c: (0, 0, 0), pipeline_mode=pl.Buffered(1))
    return pl.pallas_call(
        functools.partial(_peer_kernel, tp, ech, lb, n_chunks),
        grid=(t // tp, n_chunks + 1),
        in_specs=[pl.BlockSpec((tp, d_model), lambda i, c: (i, 0)),
                  pl.BlockSpec((1, d_model), lambda i, c: (0, 0)),
                  wq_spec, wq_spec, keys_spec, keys_spec,
                  pl.BlockSpec((ech, d_model), lambda i, c: (jnp.minimum(c, n_chunks - 1), 0)),
                  pl.BlockSpec((1, d_model, ech), lambda i, c: (jnp.maximum(c - 1, 0), 0, 0))],
        out_specs=pl.BlockSpec((tp, d_model), lambda i, c: (i, 0)),
        out_shape=jax.ShapeDtypeStruct((t, d_model), F32),
        scratch_shapes=[pltpu.VMEM((d_model, tpp), BF16),
                        pltpu.VMEM((d_model, tpp), BF16),
                        meta(F32), meta(F32),
                        pltpu.VMEM((2, N_KEYS, tpp), F32), pltpu.VMEM((2, N_KEYS, tpp), F32),
                        meta(BF16), meta(BF16),
                        pltpu.VMEM((ech, tpp), F32),
                        pltpu.VMEM((2, ech, tpp), BF16),
                        pltpu.VMEM((d_model, tpp), F32),
                        pltpu.VMEM((PEER_TOPK, lb), F32), pltpu.VMEM((PEER_TOPK, lb), F32),
                        pltpu.VMEM((_CAND_ROWS, lb), F32), pltpu.VMEM((PEER_TOPK, lb), F32)],
        compiler_params=_params("parallel", "arbitrary", flags=PEER_FLAGS),
        name="peer",
    )(x, g, wq_t[0], wq_t[1], keys[0], keys[1], u_bf16, vt_bf16)


def _final_norm_kernel(x_ref, g_ref, out_ref):
    out_ref[...] = _rmsnorm(x_ref[...], g_ref[...])


def _final_norm(x, g, *, tm):
    t, d_model = x.shape
    return pl.pallas_call(
        _final_norm_kernel,
        grid=(t // tm,),
        in_specs=[pl.BlockSpec((tm, d_model), lambda i: (i, 0)), pl.BlockSpec((1, d_model), lambda i: (0, 0))],
        out_specs=pl.BlockSpec((tm, d_model), lambda i: (i, 0)),
        out_shape=jax.ShapeDtypeStruct((t, d_model), F32),
        compiler_params=_params("parallel"),
        name="final_norm",
    )(x, g)


def _time_last(buf):
    return buf.transpose(0, 1, 3, 4, 5, 2).reshape(buf.shape[0], buf.shape[1], 2, GROUP_W, buf.shape[2])


def _time_first(buf):
    depth, batch, _, _, time = buf.shape
    return buf.reshape(depth, batch, 2, HEADS_PER_GROUP, HEAD_DIM, time).transpose(0, 1, 5, 2, 3, 4)


def kernel(x_prompt, x_sample, cache_kv_w128, cache_kv_w512, cache_kv_w2048, state_conv, norm_mix_g, norm_ffn_g, w_in, dw_kernel, dw_bias, conv_ln_g, conv_ln_b, w_conv_out, b_conv_out, w_attn_out, w_out, w_peer_q, peer_sub_keys, peer_u, peer_v, norm_final_g):
    batch, seq, d_model = x_prompt.shape
    dec_batch, ns, _ = x_sample.shape
    depth = w_in.shape[0]
    tp_tokens, ts_tokens = batch * seq, dec_batch * ns
    tm_p = 512
    cache_in = (cache_kv_w128, cache_kv_w512, cache_kv_w2048)
    for c, (window, _) in zip(cache_in, ATTN_GROUPS):
        assert c.shape[2] == window
    caches = [_time_last(c) for c in cache_in]
    assert state_conv.shape[2] == CONV_WIDTH - 1
    hist_pad = CONV_HIST - (CONV_WIDTH - 1)

    xp = x_prompt.reshape(tp_tokens, d_model)
    xs = x_sample.reshape(ts_tokens, d_model)
    tails = [jnp.zeros((depth, batch, 2, GROUP_W, min(window, seq)), F32) for (window, _) in ATTN_GROUPS]
    new_caches = [jnp.zeros(c.shape, F32) for c in caches]
    conv_p, conv_s = [], []
    vec = lambda a: a.reshape(1, -1)
    for l in range(depth):
        w_in_b = w_in[l].astype(BF16)
        wa_b, wc_b, wo_b = w_attn_out[l].astype(BF16), w_conv_out[l].astype(BF16), w_out[l].astype(BF16)
        wq_t = _split_bf16(w_peer_q[l].T)
        keys = _split_bf16(peer_sub_keys[l].reshape(2 * PEER_HEADS, N_KEYS, -1))
        u_b = peer_u[l].astype(BF16)
        vt_b = peer_v[l].reshape(-1, PEER_CHUNK, d_model).transpose(0, 2, 1).astype(BF16)
        conv_w = (dw_kernel[l], vec(dw_bias[l]), vec(conv_ln_g[l]), vec(conv_ln_b[l]))
        mix_g, ffn_g = vec(norm_mix_g[l]), vec(norm_ffn_g[l])

        outs = _in_proj(xp, mix_g, w_in_b, tm=tm_p, class_major=True, batch=batch, seq=seq, layer=l, tails=tails)
        c, ga, gb = outs[:3]
        tails = list(outs[12:])
        attn = _attn_prompt(outs[3:12], batch, seq).reshape(tp_tokens, GROUP_W)
        u = _conv_prompt(c, *conv_w, batch, seq, 512)
        xp = _merge(xp, attn, u, ga, gb, wa_b, wc_b, vec(b_conv_out[l]), wo_b, tm=tm_p)
        xp = _peer(xp, ffn_g, wq_t, keys, u_b, vt_b, tp=512)
        conv_p.append(c.reshape(batch, seq, GROUP_W)[:, seq - (CONV_WIDTH - 1):])

        qkv, c, ga, gb = _in_proj(xs, mix_g, w_in_b, tm=ts_tokens, class_major=False)
        attn, *new_caches = _attn_sample(qkv, caches, new_caches, l, dec_batch, ns)
        hist = jnp.pad(state_conv[l], ((0, 0), (hist_pad, 0), (0, 0)))
        u = _conv_sample(c, hist, *conv_w, dec_batch, ns)
        xs = _merge(xs, attn, u, ga, gb, wa_b, wc_b, vec(b_conv_out[l]), wo_b, tm=ts_tokens)
        xs = _peer(xs, ffn_g, wq_t, keys, u_b, vt_b, tp=ts_tokens)
        full = jnp.concatenate([state_conv[l], c.reshape(dec_batch, ns, GROUP_W)], axis=1)
        conv_s.append(full[:, full.shape[1] - (CONV_WIDTH - 1):])

    y_prompt = _final_norm(xp, vec(norm_final_g), tm=tm_p).reshape(batch, seq, d_model)
    y_sample = _final_norm(xs, vec(norm_final_g), tm=ts_tokens).reshape(dec_batch, ns, d_model)
    return (y_prompt, y_sample,
            _time_first(tails[0]), _time_first(tails[1]), _time_first(tails[2]), jnp.stack(conv_p),
            _time_first(new_caches[0]), _time_first(new_caches[1]), _time_first(new_caches[2]), jnp.stack(conv_s))
```

```python
import functools

import jax
import jax.numpy as jnp
from jax import lax
from jax.experimental import pallas as pl
from jax.experimental.pallas import tpu as pltpu

F32 = jnp.float32
BF16 = jnp.bfloat16

HEAD_DIM = 64
HEADS_PER_GROUP = 8
ATTN_GROUPS = ((128, 1), (512, 4), (2048, 16))
N_GROUPS = len(ATTN_GROUPS)
GROUP_W = HEADS_PER_GROUP * HEAD_DIM
QKV_W = N_GROUPS * GROUP_W
ATTN_BLOCK = 128
ATTN_SCALE = HEAD_DIM ** -0.5
CONV_WIDTH = 31
CONV_HIST = 32
N_KEYS = 128
PEER_HEADS = 8
PEER_TOPK = 16
NORM_EPS = 1e-6
NEG_INF = -1e30

VMEM_LIMIT_BYTES = 56 * 1024 * 1024
LANES = 128
HEAD_HALF = 4
ATTN_UNROLL = 4
HALF_W = HEAD_HALF * HEAD_DIM

NT_DIMS = (((1,), (1,)), ((), ()))


def _params(*sem, flags=None):
    return pltpu.CompilerParams(dimension_semantics=sem, vmem_limit_bytes=VMEM_LIMIT_BYTES, flags=flags)


def _rmsnorm(x, g):
    return x * lax.rsqrt(jnp.mean(x * x, axis=-1, keepdims=True) + NORM_EPS) * g


def _sigmoid(x):
    return 1.0 / (1.0 + jnp.exp(-x))


def _in_proj_kernel(class_major, tm, seq, x_ref, g_ref, w_ref, *refs):
    if class_major:
        c_ref, ga_ref, gb_ref = refs[3:6]
        cm_refs = refs[6:15]
        tail_refs = refs[15:18]
        scr_ref = refs[18]
        tps = seq // tm
        last_tile = pl.program_id(0) % tps == tps - 1
    else:
        qkv_ref, c_ref, ga_ref, gb_ref = refs[:4]
    h = _rmsnorm(x_ref[...], g_ref[...]).astype(BF16)
    glu_a = None
    for j in range(w_ref.shape[1] // GROUP_W):
        res = jnp.dot(h, w_ref[:, j * GROUP_W:(j + 1) * GROUP_W], preferred_element_type=F32)
        if j < 9:
            kind, g = divmod(j, N_GROUPS)
            if class_major:
                if kind > 0:
                    keep = tail_refs[g].shape[-1]
                    if min(ATTN_GROUPS[g][0], seq) == seq:
                        tail_refs[g][0, 0, kind - 1] = res.T
                    else:
                        @pl.when(last_tile)
                        def _(res=res, g=g, kind=kind, keep=keep):
                            tail_refs[g][0, 0, kind - 1] = res[tm - keep:, :].T
                val = res * ATTN_SCALE if kind == 0 else res
                d = ATTN_GROUPS[g][1]
                cm = cm_refs[kind * N_GROUPS + g]
                if d == 1:
                    cm[0, 0] = val.astype(BF16)
                else:
                    for t in range(GROUP_W // LANES):
                        scr_ref[t] = val[:, t * LANES:(t + 1) * LANES]
                    for r in range(d):
                        cm[0, r] = jnp.concatenate(
                            [scr_ref[t, pl.ds(r, tm // d, stride=d), :] for t in range(GROUP_W // LANES)], axis=1).astype(BF16)
            else:
                qkv_ref[:, j * GROUP_W:(j + 1) * GROUP_W] = res
        elif j == 9:
            glu_a = res
        elif j == 10:
            c_ref[...] = glu_a * _sigmoid(res)
        elif j < 13:
            ga_ref[:, (j - 11) * GROUP_W:(j - 10) * GROUP_W] = _sigmoid(res)
        else:
            gb_ref[:, (j - 13) * GROUP_W:(j - 12) * GROUP_W] = _sigmoid(res)


def _in_proj(x, g, w_bf16, *, tm, class_major, batch=None, seq=None, layer=None, tails=None):
    t, d_model = x.shape
    in_w = w_bf16.shape[1]
    assert t % tm == 0 and in_w == 3 * QKV_W + 2 * GROUP_W + 2 * d_model and d_model == 2 * GROUP_W
    nt = t // tm
    row = lambda i: (i, 0)
    in_specs = [
        pl.BlockSpec((tm, d_model), row),
        pl.BlockSpec((1, d_model), lambda i: (0, 0)),
        pl.BlockSpec((d_model, in_w), lambda i: (0, 0), pipeline_mode=pl.Buffered(1)),
    ]
    common_shapes = [
        jax.ShapeDtypeStruct((t, GROUP_W), F32),
        jax.ShapeDtypeStruct((t, d_model), F32),
        jax.ShapeDtypeStruct((t, d_model), F32),
    ]
    common_specs = [pl.BlockSpec((tm, GROUP_W), row), pl.BlockSpec((tm, d_model), row), pl.BlockSpec((tm, d_model), row)]
    aliases = {}
    args = (x, g, w_bf16)
    if class_major:
        assert seq % tm == 0
        tps = seq // tm
        in_specs += [pl.BlockSpec(memory_space=pl.ANY)] * N_GROUPS
        args += tuple(tails)
        out_shape = list(common_shapes)
        out_specs = list(common_specs)
        for _kind in range(3):
            for (_, d) in ATTN_GROUPS:
                assert tm % (d * 16) == 0
                out_shape.append(jax.ShapeDtypeStruct((batch, d, seq // d, GROUP_W), BF16))
                out_specs.append(pl.BlockSpec((1, d, tm // d, GROUP_W), lambda i, tps=tps: (i // tps, 0, i % tps, 0)))
        for g_idx, ((window, _), tail) in enumerate(zip(ATTN_GROUPS, tails)):
            keep = min(window, seq)
            assert tail.shape[1:] == (batch, 2, GROUP_W, keep) and (keep == seq or (keep <= tm and keep % LANES == 0))
            aliases[3 + g_idx] = len(out_shape)
            out_shape.append(jax.ShapeDtypeStruct(tail.shape, F32))
            if keep == seq:
                out_specs.append(pl.BlockSpec((1, 1, 2, GROUP_W, tm), lambda i, tps=tps: (layer, i // tps, 0, 0, i % tps)))
            else:
                out_specs.append(pl.BlockSpec((1, 1, 2, GROUP_W, keep), lambda i, tps=tps: (layer, i // tps, 0, 0, 0)))
        scratch = [pltpu.VMEM((GROUP_W // LANES, tm, LANES), F32)]
    else:
        out_shape = [jax.ShapeDtypeStruct((t, 3 * QKV_W), F32)] + common_shapes
        out_specs = [pl.BlockSpec((tm, 3 * QKV_W), row)] + common_specs
        scratch = []
    return pl.pallas_call(
        functools.partial(_in_proj_kernel, class_major, tm, seq),
        grid=(nt,),
        in_specs=in_specs,
        out_specs=out_specs,
        out_shape=out_shape,
        scratch_shapes=scratch,
        input_output_aliases=aliases,
        compiler_params=_params("arbitrary"),
        name="in_proj_cm" if class_major else "in_proj_nat",
    )(*args)


def _attend_heads(qb, kb, vb, mask, n_heads):
    outs, lses = [], []
    for h in range(n_heads):
        sl = slice(h * HEAD_DIM, (h + 1) * HEAD_DIM)
        s = lax.dot_general(qb[:, sl], kb[:, sl], NT_DIMS, preferred_element_type=F32)
        s = jnp.where(mask, s, NEG_INF)
        m = jnp.max(s, axis=-1, keepdims=True)
        p = jnp.exp(s - m)
        den = jnp.sum(p, axis=-1, keepdims=True)
        o = jnp.dot(p.astype(BF16), vb[:, sl], preferred_element_type=F32) / den
        outs.append(o)
        lses.append(jnp.broadcast_to(m + jnp.log(den), o.shape))
    return jnp.concatenate(outs, axis=1), jnp.concatenate(lses, axis=1)


def _attn_prompt_kernel(seq, *refs):
    qkv_refs = refs[:9]
    out_ref = refs[9]
    ocm_ref, lcm_ref, m_ref, n_ref, d_ref = refs[10:]
    n_lt = HALF_W // LANES
    dist = (lax.broadcasted_iota(jnp.int32, (ATTN_BLOCK, 2 * ATTN_BLOCK), 0) + ATTN_BLOCK
            - lax.broadcasted_iota(jnp.int32, (ATTN_BLOCK, 2 * ATTN_BLOCK), 1))
    band_mask = (dist >= 0) & (dist <= ATTN_BLOCK)
    causal_mask = (lax.broadcasted_iota(jnp.int32, (ATTN_BLOCK, ATTN_BLOCK), 0)
                   >= lax.broadcasted_iota(jnp.int32, (ATTN_BLOCK, ATTN_BLOCK), 1))

    def put(ref, rows, val):
        for t in range(n_lt):
            ref[t, rows, :] = val[:, t * LANES:(t + 1) * LANES]

    for g, (window, d) in enumerate(ATTN_GROUPS):
        assert window // d == ATTN_BLOCK
        q_ref, k_ref, v_ref = qkv_refs[g], qkv_refs[N_GROUPS + g], qkv_refs[2 * N_GROUPS + g]
        sd = seq // d
        nb = sd // ATTN_BLOCK

        def block(r, i, q_ref=q_ref, k_ref=k_ref, v_ref=v_ref, sd=sd):
            if isinstance(i, int) and i == 0:
                q0 = 0
                o, l = _attend_heads(q_ref[0, r, 0:ATTN_BLOCK, :], k_ref[0, r, 0:ATTN_BLOCK, :],
                                     v_ref[0, r, 0:ATTN_BLOCK, :], causal_mask, HEAD_HALF)
            else:
                q0, k0 = i * ATTN_BLOCK, (i - 1) * ATTN_BLOCK
                if not isinstance(i, int):
                    q0, k0 = pl.multiple_of(q0, ATTN_BLOCK), pl.multiple_of(k0, ATTN_BLOCK)
                o, l = _attend_heads(q_ref[0, r, pl.ds(q0, ATTN_BLOCK), :], k_ref[0, r, pl.ds(k0, 2 * ATTN_BLOCK), :],
                                     v_ref[0, r, pl.ds(k0, 2 * ATTN_BLOCK), :], band_mask, HEAD_HALF)
            dst = r * sd + q0
            if not isinstance(dst, int):
                dst = pl.multiple_of(dst, ATTN_BLOCK)
            put(ocm_ref, pl.ds(dst, ATTN_BLOCK), o)
            put(lcm_ref, pl.ds(dst, ATTN_BLOCK), l)

        if nb == 1:
            assert d % ATTN_UNROLL == 0

            def classes(it, carry, block=block):
                for u in range(ATTN_UNROLL):
                    block(it * ATTN_UNROLL + u, 0)
                return carry

            lax.fori_loop(0, d // ATTN_UNROLL, classes, 0)
        else:
            assert nb % ATTN_UNROLL == 0

            def one_class(r, carry, block=block, nb=nb):
                for i in range(ATTN_UNROLL):
                    block(r, i)

                def later_blocks(it, c2):
                    for u in range(ATTN_UNROLL):
                        block(r, (it + 1) * ATTN_UNROLL + u)
                    return c2

                if nb > ATTN_UNROLL:
                    lax.fori_loop(0, nb // ATTN_UNROLL - 1, later_blocks, 0)
                return carry

            if d == 1:
                one_class(0, 0)
            else:
                lax.fori_loop(0, d, one_class, 0)

        if g == 0:
            assert d == 1
            m_ref[...] = lcm_ref[...]
            n_ref[...] = ocm_ref[...]
            d_ref[...] = jnp.ones(d_ref.shape, F32)
        else:
            for r in range(d):
                rows = pl.ds(r, sd, stride=d)
                for t in range(n_lt):
                    o_r = ocm_ref[t, r * sd:(r + 1) * sd, :]
                    l_r = lcm_ref[t, r * sd:(r + 1) * sd, :]
                    m_old = m_ref[t, rows, :]
                    m_new = jnp.maximum(m_old, l_r)
                    a = jnp.exp(m_old - m_new)
                    b = jnp.exp(l_r - m_new)
                    n_ref[t, rows, :] = n_ref[t, rows, :] * a + o_r * b
                    d_ref[t, rows, :] = d_ref[t, rows, :] * a + b
                    m_ref[t, rows, :] = m_new
    for t in range(n_lt):
        out_ref[0, :, t * LANES:(t + 1) * LANES] = (n_ref[t] / d_ref[t]).astype(BF16)


def _attn_prompt(cm, batch, seq):
    in_specs = []
    for _kind in range(3):
        for (_, d) in ATTN_GROUPS:
            assert seq % (d * ATTN_BLOCK) == 0
            in_specs.append(pl.BlockSpec((1, d, seq // d, HALF_W), lambda b, hh: (b, 0, 0, hh)))
    return pl.pallas_call(
        functools.partial(_attn_prompt_kernel, seq),
        grid=(batch, GROUP_W // HALF_W),
        in_specs=in_specs,
        out_specs=pl.BlockSpec((1, seq, HALF_W), lambda b, hh: (b, 0, hh)),
        out_shape=jax.ShapeDtypeStruct((batch, seq, GROUP_W), BF16),
        scratch_shapes=[pltpu.VMEM((HALF_W // LANES, seq, LANES), F32) for _ in range(5)],
        compiler_params=_params("parallel", "parallel"),
        name="attn_prompt",
    )(*cm)


def _attn_sample_kernel(ns, qkv_ref, c0_ref, c1_ref, c2_ref, _n0, _n1, _n2, out_ref, new0_ref, new1_ref, new2_ref):
    cache_refs = (c0_ref, c1_ref, c2_ref)
    new_refs = (new0_ref, new1_ref, new2_ref)
    qkv = qkv_ref[...]
    row_pad = jnp.zeros((LANES - ns, GROUP_W), F32)
    lane = lax.broadcasted_iota(jnp.int32, (HEADS_PER_GROUP, GROUP_W), 1)
    hrow = lax.broadcasted_iota(jnp.int32, (HEADS_PER_GROUP, GROUP_W), 0)
    head_mask = (lane // HEAD_DIM == hrow).astype(F32)
    nr = ns * HEADS_PER_GROUP
    n_new = 16
    assert ns <= n_new
    zpad = jnp.zeros((n_new - ns, GROUP_W), F32)
    outs, lses = [], []
    for g, (window, d) in enumerate(ATTN_GROUPS):
        assert window % d == 0 and d & (d - 1) == 0 and cache_refs[g].shape[2:] == (2, GROUP_W, window)
        k_t = cache_refs[g][0, 0, 0].astype(BF16)
        v_t = cache_refs[g][0, 0, 1].astype(BF16)
        q = qkv[:, g * GROUP_W:(g + 1) * GROUP_W] * ATTN_SCALE
        k_new32 = qkv[:, QKV_W + g * GROUP_W:QKV_W + (g + 1) * GROUP_W]
        v_new32 = qkv[:, 2 * QKV_W + g * GROUP_W:2 * QKV_W + (g + 1) * GROUP_W]
        for kv, new32 in enumerate((k_new32, v_new32)):
            new_t = jnp.concatenate([new32, row_pad], axis=0).T
            new_refs[g][0, 0, kv] = jnp.concatenate([cache_refs[g][0, 0, kv][:, ns:], new_t[:, :ns]], axis=1)
        k_new = jnp.concatenate([k_new32, zpad], axis=0).astype(BF16)
        v_new = jnp.concatenate([v_new32, zpad], axis=0).astype(BF16)
        q_exp = jnp.concatenate([q[n:n + 1] * head_mask for n in range(ns)], axis=0).astype(BF16)
        s_c = jnp.dot(q_exp, k_t, preferred_element_type=F32)
        s_n = lax.dot_general(q_exp, k_new, NT_DIMS, preferred_element_type=F32)
        delta_c = (lax.broadcasted_iota(jnp.int32, (nr, window), 1)
                   - lax.broadcasted_iota(jnp.int32, (nr, window), 0) // HEADS_PER_GROUP)
        delta_n = (lax.broadcasted_iota(jnp.int32, (nr, n_new), 0) // HEADS_PER_GROUP
                   - lax.broadcasted_iota(jnp.int32, (nr, n_new), 1))
        s_c = jnp.where((delta_c >= 0) & ((delta_c & (d - 1)) == 0), s_c, NEG_INF)
        s_n = jnp.where((delta_n >= 0) & ((delta_n & (d - 1)) == 0), s_n, NEG_INF)
        m = jnp.maximum(jnp.max(s_c, axis=-1, keepdims=True), jnp.max(s_n, axis=-1, keepdims=True))
        p_c = jnp.exp(s_c - m)
        p_n = jnp.exp(s_n - m)
        den = jnp.sum(p_c, axis=-1, keepdims=True) + jnp.sum(p_n, axis=-1, keepdims=True)
        pv = (lax.dot_general(p_c.astype(BF16), v_t, NT_DIMS, preferred_element_type=F32)
              + jnp.dot(p_n.astype(BF16), v_new, preferred_element_type=F32)) / den
        lse = jnp.broadcast_to(m + jnp.log(den), pv.shape)
        o_rows, l_rows = [], []
        for n in range(ns):
            blk = slice(n * HEADS_PER_GROUP, (n + 1) * HEADS_PER_GROUP)
            o_rows.append(jnp.sum(pv[blk] * head_mask, axis=0, keepdims=True))
            l_rows.append(jnp.sum(lse[blk] * head_mask, axis=0, keepdims=True))
        outs.append(jnp.concatenate(o_rows, axis=0))
        lses.append(jnp.concatenate(l_rows, axis=0))
    m = jnp.maximum(jnp.maximum(lses[0], lses[1]), lses[2])
    es = [jnp.exp(l - m) for l in lses]
    num = es[0] * outs[0] + es[1] * outs[1] + es[2] * outs[2]
    out_ref[...] = num / (es[0] + es[1] + es[2])


def _attn_sample(qkv, caches, new_caches, layer, dec_batch, ns):
    assert ns % 8 == 0
    in_specs = [pl.BlockSpec((ns, 3 * QKV_W), lambda b: (b, 0))]
    out_specs = [pl.BlockSpec((ns, GROUP_W), lambda b: (b, 0))]
    out_shape = [jax.ShapeDtypeStruct((dec_batch * ns, GROUP_W), F32)]
    for c in caches:
        spec = pl.BlockSpec((1, 1) + c.shape[2:], lambda b, layer=layer: (layer, b, 0, 0, 0))
        in_specs.append(spec)
        out_specs.append(spec)
        out_shape.append(jax.ShapeDtypeStruct(c.shape, F32))
    in_specs += [pl.BlockSpec(memory_space=pl.ANY)] * len(new_caches)
    n_in = 1 + len(caches)
    return pl.pallas_call(
        functools.partial(_attn_sample_kernel, ns),
        grid=(dec_batch,),
        in_specs=in_specs,
        out_specs=out_specs,
        out_shape=out_shape,
        input_output_aliases={n_in + g: 1 + g for g in range(len(new_caches))},
        compiler_params=_params("arbitrary"),
        name="attn_sample",
    )(qkv, *caches, *new_caches)


def _conv_kernel(tc, has_hist, *refs):
    if has_hist:
        c_ref, hist_ref, dw_ref, bias_ref, lng_ref, lnb_ref, out_ref, ext_ref = refs
        ext_ref[0:CONV_HIST, :] = hist_ref[0]
        ext_ref[CONV_HIST:, :] = c_ref[...]
    else:
        c_ref, dw_ref, bias_ref, lng_ref, lnb_ref, out_ref, ext_ref = refs
        i = pl.program_id(1)
        t0 = pl.multiple_of(i * tc, 8)
        prev = pl.multiple_of(jnp.maximum(t0 - CONV_HIST, 0), 8)
        hist = c_ref[pl.ds(prev, CONV_HIST), :]
        ext_ref[0:CONV_HIST, :] = jnp.where(i > 0, hist, 0.0)
        ext_ref[CONV_HIST:, :] = c_ref[pl.ds(t0, tc), :]
    off = CONV_HIST - (CONV_WIDTH - 1)
    acc = ext_ref[pl.ds(off, tc), :] * dw_ref[0:1, :]
    for k in range(1, CONV_WIDTH):
        acc = acc + ext_ref[pl.ds(off + k, tc), :] * dw_ref[k:k + 1, :]
    y = acc + bias_ref[...]
    mu = jnp.mean(y, axis=-1, keepdims=True)
    yc = y - mu
    var = jnp.mean(yc * yc, axis=-1, keepdims=True)
    y = yc * lax.rsqrt(var + NORM_EPS) * lng_ref[...] + lnb_ref[...]
    out_ref[...] = y * _sigmoid(y)


def _conv_prompt(c, dw, bias, lng, lnb, batch, seq, tc):
    assert seq % tc == 0 and tc % 8 == 0
    nt = seq // tc
    vec = lambda n: pl.BlockSpec((1, n), lambda b, i: (0, 0))
    return pl.pallas_call(
        functools.partial(_conv_kernel, tc, False),
        grid=(batch, nt),
        in_specs=[pl.BlockSpec((seq, GROUP_W), lambda b, i: (b, 0)),
                  pl.BlockSpec((CONV_WIDTH, GROUP_W), lambda b, i: (0, 0)), vec(GROUP_W), vec(GROUP_W), vec(GROUP_W)],
        out_specs=pl.BlockSpec((tc, GROUP_W), lambda b, i, nt=nt: (b * nt + i, 0)),
        out_shape=jax.ShapeDtypeStruct((batch * seq, GROUP_W), F32),
        scratch_shapes=[pltpu.VMEM((tc + CONV_HIST, GROUP_W), F32)],
        compiler_params=_params("parallel", "arbitrary"),
        name="conv_prompt",
    )(c, dw, bias, lng, lnb)


def _conv_sample(c, hist, dw, bias, lng, lnb, dec_batch, ns):
    vec = lambda n: pl.BlockSpec((1, n), lambda b: (0, 0))
    return pl.pallas_call(
        functools.partial(_conv_kernel, ns, True),
        grid=(dec_batch,),
        in_specs=[pl.BlockSpec((ns, GROUP_W), lambda b: (b, 0)),
                  pl.BlockSpec((1, CONV_HIST, GROUP_W), lambda b: (b, 0, 0)),
                  pl.BlockSpec((CONV_WIDTH, GROUP_W), lambda b: (0, 0)), vec(GROUP_W), vec(GROUP_W), vec(GROUP_W)],
        out_specs=pl.BlockSpec((ns, GROUP_W), lambda b: (b, 0)),
        out_shape=jax.ShapeDtypeStruct((dec_batch * ns, GROUP_W), F32),
        scratch_shapes=[pltpu.VMEM((ns + CONV_HIST, GROUP_W), F32)],
        compiler_params=_params("parallel"),
        name="conv_sample",
    )(c, hist, dw, bias, lng, lnb)


def _merge_kernel(x_ref, attn_ref, u_ref, ga_ref, gb_ref, wa_ref, wc_ref, bc_ref, wo_ref, out_ref):
    y_a = jnp.dot(attn_ref[...].astype(BF16), wa_ref[...], preferred_element_type=F32)
    y_b = jnp.dot(u_ref[...].astype(BF16), wc_ref[...], preferred_element_type=F32) + bc_ref[...]
    y = (ga_ref[...] * y_a + gb_ref[...] * y_b).astype(BF16)
    out_ref[...] = x_ref[...] + jnp.dot(y, wo_ref[...], preferred_element_type=F32)


def _merge(x, attn, u, ga, gb, wa, wc, bc, wo, *, tm):
    t, d_model = x.shape
    assert t % tm == 0
    row = lambda i: (i, 0)
    const = lambda i: (0, 0)
    return pl.pallas_call(
        _merge_kernel,
        grid=(t // tm,),
        in_specs=[pl.BlockSpec((tm, d_model), row), pl.BlockSpec((tm, GROUP_W), row), pl.BlockSpec((tm, GROUP_W), row),
                  pl.BlockSpec((tm, d_model), row), pl.BlockSpec((tm, d_model), row),
                  pl.BlockSpec((GROUP_W, d_model), const), pl.BlockSpec((GROUP_W, d_model), const),
                  pl.BlockSpec((1, d_model), const), pl.BlockSpec((d_model, d_model), const)],
        out_specs=pl.BlockSpec((tm, d_model), row),
        out_shape=jax.ShapeDtypeStruct((t, d_model), F32),
        compiler_params=_params("parallel"),
        name="merge",
    )(x, attn, u, ga, gb, wa, wc, bc, wo)


_NO_RANK = 64.0


def _top_values(s, out_ref, n, want_rank=False):
    cur = s
    rank = jnp.full(s.shape, _NO_RANK, F32) if want_rank else None
    for j in range(n):
        m = jnp.max(cur, axis=0, keepdims=True)
        out_ref[j:j + 1, :] = m
        hit = cur == m
        if want_rank:
            rank = jnp.where(hit, float(j), rank)
        if j + 1 < n:
            cur = jnp.where(hit, -jnp.inf, cur)
    return rank


def _split_bf16(x):
    hi = x.astype(BF16)
    return hi, (x - hi.astype(F32)).astype(BF16)


def _dot_bf16x3(a_hi, a_lo, b_hi, b_lo):
    d = functools.partial(jnp.dot, preferred_element_type=F32)
    return d(a_hi, b_hi) + (d(a_hi, b_lo) + d(a_lo, b_hi))


_CAND = [(i, PEER_TOPK // (i + 1)) for i in range(PEER_TOPK)]
_CAND_ROWS = 64
PEER_SLAB = 256
GATE_ROWS = 16


def _peer_kernel(tp, ech, lb, n_chunks, x_ref, g_ref, wqh_ref, wql_ref, kh_ref, kl_ref, u_ref, vt_ref, out_ref,
                 ht_ref, htl_ref, cnt_ref, e1_ref, sa_ref, sb_ref, rank_ref, e2_ref, at_ref, gt_ref, acc_ref,
                 a_ref, b_ref, cand_ref, top_ref):
    c = pl.program_id(1)
    n_lb = tp // lb

    @pl.when(c == 0)
    def _route():
        h2 = _rmsnorm(x_ref[...], g_ref[...])
        ht = h2.T
        ht_hi, ht_lo = _split_bf16(ht)
        ht_ref[...] = ht_hi
        htl_ref[...] = ht_lo
        acc_ref[...] = jnp.zeros(acc_ref.shape, F32)
        gt_ref[1] = jnp.zeros(gt_ref.shape[1:], BF16)

        def scores(h, s_ref):
            for half in range(2):
                k = 2 * h + half
                rows = pl.ds(pl.multiple_of(k * N_KEYS, N_KEYS), N_KEYS)
                q = _dot_bf16x3(wqh_ref[rows, :], wql_ref[rows, :], ht_ref[...], htl_ref[...])
                q_hi, q_lo = _split_bf16(q)
                s_ref[half] = _dot_bf16x3(kh_ref[k], kl_ref[k], q_hi, q_lo)

        def select(h, s_ref):
            for j in range(n_lb):
                cols = slice(j * lb, (j + 1) * lb)
                s1 = s_ref[0, :, cols]
                s2 = s_ref[1, :, cols]
                _top_values(s1, a_ref, PEER_TOPK)
                rank2 = _top_values(s2, b_ref, PEER_TOPK, want_rank=True)
                cand_ref[...] = jnp.full(cand_ref.shape, -jnp.inf, F32)
                row = 0
                for (i, n) in _CAND:
                    cand_ref[row:row + n, :] = a_ref[i:i + 1, :] + b_ref[0:n, :]
                    row += n
                _top_values(cand_ref[...], top_ref, PEER_TOPK)
                top = top_ref[...]
                tau = top[PEER_TOPK - 1:PEER_TOPK, :]
                z = jnp.sum(jnp.exp(top - top[0:1, :]), axis=0, keepdims=True)
                a0, a15 = a_ref[0:1, :], a_ref[PEER_TOPK - 1:PEER_TOPK, :]
                b0 = b_ref[0:1, :]
                cnt = jnp.zeros(s1.shape, F32)
                for r in range(PEER_TOPK):
                    cnt = cnt + jnp.where(s1 + b_ref[r:r + 1, :] >= tau, 1.0, 0.0)
                cnt_ref[h, :, cols] = jnp.where(s1 >= a15, cnt, 0.0)
                e1_ref[h, :, cols] = jnp.exp(s1 - a0)
                rank_ref[h, :, cols] = rank2.astype(BF16)
                e2_ref[h, :, cols] = (jnp.exp(s2 - b0) / z).astype(BF16)

        scores(0, sa_ref)

        def head_pair(hp, carry):
            h = 2 * hp
            scores(h + 1, sb_ref)
            select(h, sa_ref)
            scores(jnp.minimum(h + 2, PEER_HEADS - 1), sa_ref)
            select(h + 1, sb_ref)
            return carry

        lax.fori_loop(0, PEER_HEADS // 2, head_pair, 0)

    n_slabs = ech // PEER_SLAB
    d_part = acc_ref.shape[0] // n_slabs

    cur = c % 2
    prev = 1 - cur

    def output_part(p):
        rows = slice(p * d_part, (p + 1) * d_part)
        acc_ref[rows, :] += jnp.dot(vt_ref[0, rows, :], gt_ref[prev], preferred_element_type=F32)

    def pre_activations(si):
        slab = slice(si * PEER_SLAB, (si + 1) * PEER_SLAB)
        at_ref[slab, :] = jnp.dot(u_ref[slab, :], ht_ref[...], preferred_element_type=F32)

    def gate_slab(si):
        i1_rows = pl.ds(pl.multiple_of(c * (ech // N_KEYS), 8), ech // N_KEYS)
        s0 = si * PEER_SLAB
        for i1l in range(s0 // N_KEYS, (s0 + PEER_SLAB) // N_KEYS):
            rows = slice(i1l * N_KEYS, (i1l + 1) * N_KEYS)
            for j in range(n_lb):
                cols = slice(j * lb, (j + 1) * lb)
                w = jnp.zeros((N_KEYS, lb), BF16)
                for h in range(PEER_HEADS):
                    cnt_b = jnp.broadcast_to(cnt_ref[h, i1_rows, cols][i1l:i1l + 1], (N_KEYS, lb)).astype(BF16)
                    e1_b = jnp.broadcast_to(e1_ref[h, i1_rows, cols][i1l:i1l + 1], (N_KEYS, lb)).astype(BF16)
                    keep = rank_ref[h, :, cols] < cnt_b
                    w = w + jnp.where(keep, e2_ref[h, :, cols] * e1_b, jnp.zeros((), BF16))
                a = at_ref[rows, cols]
                act = 0.5 * a * (1.0 + lax.erf(a * (0.5 ** 0.5)))
                gt_ref[cur, rows, cols] = w * act.astype(BF16)

    @pl.when(c < n_chunks)
    def _chunk():
        pre_activations(0)
        for si in range(n_slabs):
            if si + 1 < n_slabs:
                pre_activations(si + 1)
            output_part(si)
            gate_slab(si)

    @pl.when(c == n_chunks)
    def _drain():
        for p in range(n_slabs):
            output_part(p)
        out_ref[...] = x_ref[...] + acc_ref[...].T


PEER_CHUNK = 8 * N_KEYS
PEER_FLAGS = None


def _peer(x, g, wq_t, keys, u_bf16, vt_bf16, *, tp, lb=LANES):
    t, d_model = x.shape
    n_exp = u_bf16.shape[0]
    ech = PEER_CHUNK
    n_chunks = n_exp // ech
    assert t % tp == 0 and n_exp % ech == 0 and tp % lb == 0 and n_exp == N_KEYS * N_KEYS
    assert vt_bf16.shape == (n_chunks, d_model, ech)
    q_w = wq_t[0].shape[0]
    assert q_w == PEER_HEADS * 2 * N_KEYS and ech % PEER_SLAB == 0 and d_model % (ech // PEER_SLAB) == 0
    meta = lambda dt: pltpu.VMEM((PEER_HEADS, N_KEYS, tp), dt)
    wq_spec = pl.BlockSpec((q_w, d_model), lambda i, c: (0, 0), pipeline_mode=pl.Buffered(1))
    keys_spec = pl.BlockSpec((2 * PEER_HEADS, N_KEYS, N_KEYS), lambda i, c: (0, 0, 0), pipeline_mode=pl.Buffered(1))
    return pl.pallas_call(
        functools.partial(_peer_kernel, tp, ech, lb, n_chunks),
        grid=(t // tp, n_chunks + 1),
        in_specs=[pl.BlockSpec((tp, d_model), lambda i, c: (i, 0)),
                  pl.BlockSpec((1, d_model), lambda i, c: (0, 0)),
                  wq_spec, wq_spec, keys_spec, keys_spec,
                  pl.BlockSpec((ech, d_model), lambda i, c: (jnp.minimum(c, n_chunks - 1), 0)),
                  pl.BlockSpec((1, d_model, ech), lambda i, c: (jnp.maximum(c - 1, 0), 0, 0))],
        out_specs=pl.BlockSpec((tp, d_model), lambda i, c: (i, 0)),
        out_shape=jax.ShapeDtypeStruct((t, d_model), F32),
        scratch_shapes=[pltpu.VMEM((d_model, tp), BF16),
                        pltpu.VMEM((d_model, tp), BF16),
                        meta(F32), meta(F32),
                        pltpu.VMEM((2, N_KEYS, tp), F32), pltpu.VMEM((2, N_KEYS, tp), F32),
                        meta(BF16), meta(BF16),
                        pltpu.VMEM((ech, tp), F32),
                        pltpu.VMEM((2, ech, tp), BF16),
                        pltpu.VMEM((d_model, tp), F32),
                        pltpu.VMEM((PEER_TOPK, lb), F32), pltpu.VMEM((PEER_TOPK, lb), F32),
                        pltpu.VMEM((_CAND_ROWS, lb), F32), pltpu.VMEM((PEER_TOPK, lb), F32)],
        compiler_params=_params("parallel", "arbitrary", flags=PEER_FLAGS),
        name="peer",
    )(x, g, wq_t[0], wq_t[1], keys[0], keys[1], u_bf16, vt_bf16)


def _final_norm_kernel(x_ref, g_ref, out_ref):
    out_ref[...] = _rmsnorm(x_ref[...], g_ref[...])


def _final_norm(x, g, *, tm):
    t, d_model = x.shape
    return pl.pallas_call(
        _final_norm_kernel,
        grid=(t // tm,),
        in_specs=[pl.BlockSpec((tm, d_model), lambda i: (i, 0)), pl.BlockSpec((1, d_model), lambda i: (0, 0))],
        out_specs=pl.BlockSpec((tm, d_model), lambda i: (i, 0)),
        out_shape=jax.ShapeDtypeStruct((t, d_model), F32),
        compiler_params=_params("parallel"),
        name="final_norm",
    )(x, g)


def _unwritten(shape):
    return pl.pallas_call(
        lambda o_ref: None,
        out_shape=jax.ShapeDtypeStruct(shape, F32),
        out_specs=pl.BlockSpec(memory_space=pl.ANY),
        name="unwritten",
    )()


def _time_last(buf):
    return buf.transpose(0, 1, 3, 4, 5, 2).reshape(buf.shape[0], buf.shape[1], 2, GROUP_W, buf.shape[2])


def _time_first(buf):
    depth, batch, _, _, time = buf.shape
    return buf.reshape(depth, batch, 2, HEADS_PER_GROUP, HEAD_DIM, time).transpose(0, 1, 5, 2, 3, 4)


def kernel(x_prompt, x_sample, cache_kv_w128, cache_kv_w512, cache_kv_w2048, state_conv, norm_mix_g, norm_ffn_g, w_in, dw_kernel, dw_bias, conv_ln_g, conv_ln_b, w_conv_out, b_conv_out, w_attn_out, w_out, w_peer_q, peer_sub_keys, peer_u, peer_v, norm_final_g):
    batch, seq, d_model = x_prompt.shape
    dec_batch, ns, _ = x_sample.shape
    depth = w_in.shape[0]
    tp_tokens, ts_tokens = batch * seq, dec_batch * ns
    tm_p = 512
    cache_in = (cache_kv_w128, cache_kv_w512, cache_kv_w2048)
    for c, (window, _) in zip(cache_in, ATTN_GROUPS):
        assert c.shape[2] == window
    caches = [_time_last(c) for c in cache_in]
    assert state_conv.shape[2] == CONV_WIDTH - 1
    hist_pad = CONV_HIST - (CONV_WIDTH - 1)

    xp = x_prompt.reshape(tp_tokens, d_model)
    xs = x_sample.reshape(ts_tokens, d_model)
    tails = [_unwritten((depth, batch, 2, GROUP_W, min(window, seq))) for (window, _) in ATTN_GROUPS]
    new_caches = [_unwritten(c.shape) for c in caches]
    conv_p, conv_s = [], []
    vec = lambda a: a.reshape(1, -1)
    for l in range(depth):
        w_in_b = w_in[l].astype(BF16)
        wa_b, wc_b, wo_b = w_attn_out[l].astype(BF16), w_conv_out[l].astype(BF16), w_out[l].astype(BF16)
        wq_t = _split_bf16(w_peer_q[l].T)
        keys = _split_bf16(peer_sub_keys[l].reshape(2 * PEER_HEADS, N_KEYS, -1))
        u_b = peer_u[l].astype(BF16)
        vt_b = peer_v[l].reshape(-1, PEER_CHUNK, d_model).transpose(0, 2, 1).astype(BF16)
        conv_w = (dw_kernel[l], vec(dw_bias[l]), vec(conv_ln_g[l]), vec(conv_ln_b[l]))
        mix_g, ffn_g = vec(norm_mix_g[l]), vec(norm_ffn_g[l])

        outs = _in_proj(xp, mix_g, w_in_b, tm=tm_p, class_major=True, batch=batch, seq=seq, layer=l, tails=tails)
        c, ga, gb = outs[:3]
        tails = list(outs[12:])
        attn = _attn_prompt(outs[3:12], batch, seq).reshape(tp_tokens, GROUP_W)
        u = _conv_prompt(c, *conv_w, batch, seq, 512)
        xp = _merge(xp, attn, u, ga, gb, wa_b, wc_b, vec(b_conv_out[l]), wo_b, tm=tm_p)
        xp = _peer(xp, ffn_g, wq_t, keys, u_b, vt_b, tp=512)
        conv_p.append(c.reshape(batch, seq, GROUP_W)[:, seq - (CONV_WIDTH - 1):])

        qkv, c, ga, gb = _in_proj(xs, mix_g, w_in_b, tm=ts_tokens, class_major=False)
        attn, *new_caches = _attn_sample(qkv, caches, new_caches, l, dec_batch, ns)
        hist = jnp.pad(state_conv[l], ((0, 0), (hist_pad, 0), (0, 0)))
        u = _conv_sample(c, hist, *conv_w, dec_batch, ns)
        xs = _merge(xs, attn, u, ga, gb, wa_b, wc_b, vec(b_conv_out[l]), wo_b, tm=ts_tokens)
        xs = _peer(xs, ffn_g, wq_t, keys, u_b, vt_b, tp=ts_tokens)
        full = jnp.concatenate([state_conv[l], c.reshape(dec_batch, ns, GROUP_W)], axis=1)
        conv_s.append(full[:, full.shape[1] - (CONV_WIDTH - 1):])

    y_prompt = _final_norm(xp, vec(norm_final_g), tm=tm_p).reshape(batch, seq, d_model)
    y_sample = _final_norm(xs, vec(norm_final_g), tm=ts_tokens).reshape(dec_batch, ns, d_model)
    return (y_prompt, y_sample,
            _time_first(tails[0]), _time_first(tails[1]), _time_first(tails[2]), jnp.stack(conv_p),
            _time_first(new_caches[0]), _time_first(new_caches[1]), _time_first(new_caches[2]), jnp.stack(conv_s))
```
